```python
import math
import jax, jax.numpy as jnp
from jax import lax
import numpy as np

D_MODEL = 2048
BATCH = 8
SEQ = 2048
DEPTH = 2

HEAD_DIM = 128
N_HEAD_SLOTS = D_MODEL // HEAD_DIM
H_GLA = (5 * N_HEAD_SLOTS) // 16
H_FOX = (5 * N_HEAD_SLOTS) // 16
H_GDN = N_HEAD_SLOTS - H_GLA - H_FOX
GDN_DK = 128
GDN_DV = 128
GLA_DK = 64
GLA_DV = 128
FOX_D = 128
GDN_CONV = 4
GLA_RANK = 16
GLA_NORMALIZER = 16.0
CHUNK = 64
Q_BLOCK = 128
D_FF = 4 * D_MODEL
FFN_CONV = 3
EPS = 1e-6

GDN_QK = H_GDN * GDN_DK
GDN_V = H_GDN * GDN_DV
GLA_QK = H_GLA * GLA_DK
GLA_V = H_GLA * GLA_DV
FOX_W = H_FOX * FOX_D
MIX_WIDTH = GDN_V + GLA_V + FOX_W
IN_WIDTHS = (GDN_QK, GDN_QK, GDN_V, GDN_V, H_GDN, H_GDN,
             GLA_QK, GLA_QK, GLA_V, GLA_V, GLA_RANK,
             FOX_W, FOX_W, FOX_W, H_FOX)
N_IN = sum(IN_WIDTHS)

kernel_name = 'hymba_style_gdn_gla_fox_convffn'


def _split_points():
    pts, acc = [], 0
    for w in IN_WIDTHS[:-1]:
        acc += w
        pts.append(acc)
    return pts


def rms_norm(x, w):
    xf = x.astype(jnp.float32)
    y = xf * lax.rsqrt(jnp.mean(xf * xf, axis=-1, keepdims=True) + EPS)
    return (y * w.astype(jnp.float32)).astype(x.dtype)


def l2norm(x):
    xf = x.astype(jnp.float32)
    return (xf * lax.rsqrt(jnp.sum(xf * xf, axis=-1, keepdims=True) + EPS)).astype(x.dtype)


def causal_depthwise_conv(x, w):
    width, ch = w.shape
    return lax.conv_general_dilated(
        x, w[:, None, :].astype(x.dtype), window_strides=(1,), padding=[(width - 1, 0)],
        dimension_numbers=('NWC', 'WIO', 'NWC'), feature_group_count=ch)


def to_heads(t, n_heads, d):
    b, s, _ = t.shape
    return t.reshape(b, s, n_heads, d).transpose(0, 2, 1, 3)


def from_heads(t):
    b, h, s, d = t.shape
    return t.transpose(0, 2, 1, 3).reshape(b, s, h * d)


def gated_delta_chunked(q, k, v, g, beta):
    dtype = v.dtype
    q, k, v, g, beta = [t.astype(jnp.float32) for t in (q, k, v, g, beta)]
    b, h, s, dk = q.shape
    dv = v.shape[-1]
    n = s // CHUNK
    rs = lambda t: t.reshape(b, h, n, CHUNK, *t.shape[3:])
    q, k, v, g, beta = rs(q), rs(k), rs(v), rs(g), rs(beta)
    gc = jnp.cumsum(g, axis=-1)
    tri_incl = jnp.tril(jnp.ones((CHUNK, CHUNK), bool))
    tri_strict = jnp.tril(jnp.ones((CHUNK, CHUNK), bool), -1)
    decay = jnp.exp(jnp.where(tri_incl, gc[..., :, None] - gc[..., None, :], -jnp.inf))
    kb = k * beta[..., None]
    a_mat = jnp.where(tri_strict, jnp.einsum('bhnid,bhnjd->bhnij', kb, k) * decay, 0.0)
    rhs = jnp.concatenate([v * beta[..., None], kb * jnp.exp(gc)[..., None]], axis=-1)
    sol = lax.linalg.triangular_solve(a_mat, rhs, left_side=True, lower=True, unit_diagonal=True)
    u, w = sol[..., :dv], sol[..., dv:]
    qk = jnp.where(tri_incl, jnp.einsum('bhnid,bhnjd->bhnij', q, k) * decay, 0.0)

    def step(state, xs):
        q_c, k_c, u_c, w_c, qk_c, g_c = xs
        v_new = u_c - jnp.einsum('bhcd,bhde->bhce', w_c, state)
        o_c = (jnp.einsum('bhcd,bhde->bhce', q_c * jnp.exp(g_c)[..., None], state)
               + jnp.einsum('bhij,bhje->bhie', qk_c, v_new))
        g_last = g_c[..., -1]
        state = (state * jnp.exp(g_last)[..., None, None]
                 + jnp.einsum('bhcd,bhce->bhde', k_c * jnp.exp(g_last[..., None] - g_c)[..., None], v_new))
        return state, o_c

    xs = tuple(jnp.moveaxis(t, 2, 0) for t in (q, k, u, w, qk, gc))
    s0 = jnp.zeros((b, h, dk, dv), jnp.float32)
    _, o = lax.scan(step, s0, xs)
    return jnp.moveaxis(o, 0, 2).reshape(b, h, s, dv).astype(dtype)


def gla_chunked(q, k, v, gk):
    dtype = v.dtype
    q, k, v, gk = [t.astype(jnp.float32) for t in (q, k, v, gk)]
    b, h, s, dk = q.shape
    dv = v.shape[-1]
    n = s // CHUNK
    rs = lambda t: t.reshape(b, h, n, CHUNK, t.shape[-1])
    q, k, v, gk = rs(q), rs(k), rs(v), rs(gk)
    bc = jnp.cumsum(gk, axis=-2)
    q_dec = q * jnp.exp(bc)
    k_inv = k * jnp.exp(-bc)
    k_to_end = k * jnp.exp(bc[..., -1:, :] - bc)
    tri_incl = jnp.tril(jnp.ones((CHUNK, CHUNK), bool))
    attn = jnp.where(tri_incl, jnp.einsum('bhnid,bhnjd->bhnij', q_dec, k_inv), 0.0)
    o_intra = jnp.einsum('bhnij,bhnje->bhnie', attn, v)
    decay_last = jnp.exp(bc[..., -1, :])

    def step(state, xs):
        q_c, k_c, v_c, d_c = xs
        o_c = jnp.einsum('bhcd,bhde->bhce', q_c, state)
        state = state * d_c[..., None] + jnp.einsum('bhcd,bhce->bhde', k_c, v_c)
        return state, o_c

    xs = tuple(jnp.moveaxis(t, 2, 0) for t in (q_dec, k_to_end, v, decay_last))
    s0 = jnp.zeros((b, h, dk, dv), jnp.float32)
    _, o_inter = lax.scan(step, s0, xs)
    o = o_intra + jnp.moveaxis(o_inter, 0, 2)
    return o.reshape(b, h, s, dv).astype(dtype)


def gdn_mixer(q, k, v, z, beta_logit, alpha_in, conv_w, a_log, dt_bias, norm_w):
    b, s, _ = q.shape
    qkv = jax.nn.silu(causal_depthwise_conv(jnp.concatenate([q, k, v], axis=-1), conv_w))
    q, k, v = jnp.split(qkv, [GDN_QK, 2 * GDN_QK], axis=-1)
    q = l2norm(to_heads(q, H_GDN, GDN_DK)) * (GDN_DK ** -0.5)
    k = l2norm(to_heads(k, H_GDN, GDN_DK))
    v = to_heads(v, H_GDN, GDN_DV)
    beta = jax.nn.sigmoid(beta_logit.astype(jnp.float32)).transpose(0, 2, 1)
    g = -(jnp.exp(a_log.astype(jnp.float32))
          * jax.nn.softplus(alpha_in.astype(jnp.float32) + dt_bias.astype(jnp.float32))).transpose(0, 2, 1)
    o = gated_delta_chunked(q, k, v, g, beta).transpose(0, 2, 1, 3)
    o = rms_norm(o, norm_w) * jax.nn.silu(z.reshape(b, s, H_GDN, GDN_DV))
    return o.reshape(b, s, GDN_V)


def gla_mixer(q, k, v, g_out, gate_lr, w_gate, b_gate, norm_w):
    b, s, _ = q.shape
    gk = jax.nn.log_sigmoid((gate_lr @ w_gate + b_gate).astype(jnp.float32)) / GLA_NORMALIZER
    q = to_heads(q, H_GLA, GLA_DK) * (GLA_DK ** -0.5)
    k = to_heads(k, H_GLA, GLA_DK)
    v = to_heads(v, H_GLA, GLA_DV)
    gk = to_heads(gk, H_GLA, GLA_DK)
    o = gla_chunked(q, k, v, gk).transpose(0, 2, 1, 3)
    o = rms_norm(o, norm_w) * jax.nn.silu(g_out.reshape(b, s, H_GLA, GLA_DV))
    return o.reshape(b, s, GLA_V)


def fox_mixer(q, k, v, f_logit, f_bias):
    s_len = q.shape[1]
    q = to_heads(q, H_FOX, FOX_D)
    k = to_heads(k, H_FOX, FOX_D)
    v = to_heads(v, H_FOX, FOX_D)
    log_f = jax.nn.log_sigmoid(f_logit.astype(jnp.float32) + f_bias.astype(jnp.float32)).transpose(0, 2, 1)
    c = jnp.cumsum(log_f, axis=-1)
    scale = FOX_D ** -0.5
    outs = []
    for blk in range(s_len // Q_BLOCK):
        lo, hi = blk * Q_BLOCK, (blk + 1) * Q_BLOCK
        scores = (jnp.einsum('bhqd,bhkd->bhqk', q[:, :, lo:hi], k[:, :, :hi]).astype(jnp.float32) * scale
                  + c[:, :, lo:hi, None] - c[:, :, None, :hi])
        causal = (lo + jnp.arange(Q_BLOCK))[:, None] >= jnp.arange(hi)[None, :]
        p = jax.nn.softmax(jnp.where(causal, scores, -jnp.inf), axis=-1).astype(v.dtype)
        outs.append(jnp.einsum('bhqk,bhkd->bhqd', p, v[:, :, :hi]))
    return from_heads(jnp.concatenate(outs, axis=2))


def hybrid_layer(x, w_in, conv_gdn, gdn_a_log, gdn_dt_bias, gdn_norm, gla_w_gate, gla_b_gate,
                 gla_norm, fox_f_bias, w_out, norm_pre_mix, norm_post_mix, norm_pre_ffn,
                 norm_post_ffn, w_up, conv_ffn, conv_ffn_bias, w_down):
    h = rms_norm(x, norm_pre_mix)
    proj = h @ w_in
    (gdn_q, gdn_k, gdn_v, gdn_z, gdn_b, gdn_a,
     gla_q, gla_k, gla_v, gla_g, gla_lr,
     fox_q, fox_k, fox_v, fox_f) = jnp.split(proj, _split_points(), axis=-1)
    o = jnp.concatenate([
        gdn_mixer(gdn_q, gdn_k, gdn_v, gdn_z, gdn_b, gdn_a, conv_gdn, gdn_a_log, gdn_dt_bias, gdn_norm),
        gla_mixer(gla_q, gla_k, gla_v, gla_g, gla_lr, gla_w_gate, gla_b_gate, gla_norm),
        fox_mixer(fox_q, fox_k, fox_v, fox_f, fox_f_bias),
    ], axis=-1)
    x = x + rms_norm(o @ w_out, norm_post_mix)
    h = rms_norm(x, norm_pre_ffn)
    u = causal_depthwise_conv(h @ w_up, conv_ffn) + conv_ffn_bias
    gate, val = jnp.split(u, 2, axis=-1)
    y = (jax.nn.gelu(gate, approximate=True) * val) @ w_down
    return x + rms_norm(y, norm_post_ffn)


def setup_inputs(seed: int = 0) -> dict:
    key = jax.random.key(seed)
    ks = jax.random.split(key, 20)
    f32 = jnp.float32
    nrm = lambda k, shape, s: jax.random.normal(k, shape, f32) * s
    gain = lambda k, n: 1.0 + 0.02 * jax.random.normal(k, (DEPTH, n), f32)
    dt = jnp.exp(jax.random.uniform(ks[4], (DEPTH, H_GDN), f32, math.log(1e-3), math.log(1e-1)))
    return {
        'x': jax.random.normal(ks[0], (BATCH, SEQ, D_MODEL), f32),
        'w_in': nrm(ks[1], (DEPTH, D_MODEL, N_IN), D_MODEL ** -0.5),
        'conv_gdn': nrm(ks[2], (DEPTH, GDN_CONV, 2 * GDN_QK + GDN_V), GDN_CONV ** -0.5),
        'gdn_a_log': jnp.log(jax.random.uniform(ks[3], (DEPTH, H_GDN), f32, 1.0, 16.0)),
        'gdn_dt_bias': dt + jnp.log(-jnp.expm1(-dt)),
        'gdn_norm': gain(ks[5], GDN_DV),
        'gla_w_gate': nrm(ks[6], (DEPTH, GLA_RANK, GLA_QK), GLA_RANK ** -0.5),
        'gla_b_gate': nrm(ks[7], (DEPTH, GLA_QK), 0.02),
        'gla_norm': gain(ks[8], GLA_DV),
        'fox_f_bias': jax.random.uniform(ks[9], (DEPTH, H_FOX), f32, 1.0, 5.0),
        'w_out': nrm(ks[10], (DEPTH, MIX_WIDTH, D_MODEL), MIX_WIDTH ** -0.5),
        'norm_pre_mix': gain(ks[11], D_MODEL),
        'norm_post_mix': gain(ks[12], D_MODEL),
        'norm_pre_ffn': gain(ks[13], D_MODEL),
        'norm_post_ffn': gain(ks[14], D_MODEL),
        'w_up': nrm(ks[15], (DEPTH, D_MODEL, 2 * D_FF), D_MODEL ** -0.5),
        'conv_ffn': nrm(ks[16], (DEPTH, FFN_CONV, 2 * D_FF), FFN_CONV ** -0.5),
        'conv_ffn_bias': nrm(ks[17], (DEPTH, 2 * D_FF), 0.02),
        'w_down': nrm(ks[18], (DEPTH, D_FF, D_MODEL), D_FF ** -0.5),
    }


def reference(x, w_in, conv_gdn, gdn_a_log, gdn_dt_bias, gdn_norm, gla_w_gate, gla_b_gate,
              gla_norm, fox_f_bias, w_out, norm_pre_mix, norm_post_mix, norm_pre_ffn,
              norm_post_ffn, w_up, conv_ffn, conv_ffn_bias, w_down):
    for i in range(DEPTH):
        x = hybrid_layer(x, w_in[i], conv_gdn[i], gdn_a_log[i], gdn_dt_bias[i], gdn_norm[i],
                         gla_w_gate[i], gla_b_gate[i], gla_norm[i], fox_f_bias[i], w_out[i],
                         norm_pre_mix[i], norm_post_mix[i], norm_pre_ffn[i], norm_post_ffn[i],
                         w_up[i], conv_ffn[i], conv_ffn_bias[i], w_down[i])
    return x
```

```python
import functools
import math

import jax
import jax.numpy as jnp
from jax import lax
from jax.experimental import pallas as pl
from jax.experimental.pallas import tpu as pltpu

F32 = jnp.float32
BF16 = jnp.bfloat16

D_MODEL = 2048
HEAD = 128
H_GDN, H_GLA, H_FOX = 6, 5, 5
GLA_DK = 64
GDN_CONV = 4
GLA_RANK = 16
GLA_NORMALIZER = 16.0
CHUNK = 64
SUPER = 256
D_FF = 4 * D_MODEL
EPS = 1e-6

GDN_W = H_GDN * HEAD
GLA_W = H_GLA * HEAD
FOX_W = H_FOX * HEAD
OFF_GDN = 0
OFF_GLA = 3200
OFF_FOX = 5760
PROJ_W = 7680
LANE_GDN_B, LANE_GDN_A, LANE_GLA_LR, LANE_FOX_F = 0, 6, 12, 28

V7X_VMEM_LIMIT = 56 * 1024 * 1024


def _params(sem, vmem=V7X_VMEM_LIMIT):
    return pltpu.CompilerParams(dimension_semantics=sem, vmem_limit_bytes=vmem)


def _dot(a, b):
    return jnp.dot(a, b, preferred_element_type=F32)


def _dot_nt(a, b):
    return lax.dot_general(a, b, (((1,), (1,)), ((), ())), preferred_element_type=F32)


def _dot_tn(a, b):
    return lax.dot_general(a, b, (((0,), (0,)), ((), ())), preferred_element_type=F32)


def _split3(x):
    x1 = x.astype(BF16)
    r = x - x1.astype(F32)
    x2 = r.astype(BF16)
    x3 = (r - x2.astype(F32)).astype(BF16)
    return x1, x2, x3


def _dot_exact_lhs(m, x):
    x1, x2, x3 = _split3(x)
    return _dot(m, x1) + _dot(m, x2) + _dot(m, x3)


def _sigmoid(x):
    return 1.0 / (1.0 + jnp.exp(-x))


def _softplus(x):
    return jnp.maximum(x, 0.0) + jnp.log1p(jnp.exp(-jnp.abs(x)))


def _log_sigmoid(x):
    return jnp.minimum(x, 0.0) - jnp.log1p(jnp.exp(-jnp.abs(x)))


def _rms(x, w):
    return x * lax.rsqrt(jnp.mean(x * x, axis=-1, keepdims=True) + EPS) * w


def _chunk_masks(n):
    row = lax.broadcasted_iota(jnp.int32, (n, n), 0)
    col = lax.broadcasted_iota(jnp.int32, (n, n), 1)
    same = (row // CHUNK) == (col // CHUNK)
    return same & (row >= col), same & (row > col)


def _rmsnorm_kernel(x_ref, w_ref, o_ref):
    o_ref[...] = _rms(x_ref[...], w_ref[...]).astype(o_ref.dtype)


def _rmsnorm(x, w, tm=512):
    n, d = x.shape
    return pl.pallas_call(
        _rmsnorm_kernel,
        grid=(n // tm,),
        in_specs=[pl.BlockSpec((tm, d), lambda i: (i, 0)), pl.BlockSpec((1, d), lambda i: (0, 0))],
        out_specs=pl.BlockSpec((tm, d), lambda i: (i, 0)),
        out_shape=jax.ShapeDtypeStruct((n, d), BF16),
        compiler_params=_params(("parallel",)),
        name="rmsnorm",
    )(x, w.reshape(1, d))


def _mm_kernel(a_ref, b_ref, o_ref):
    o_ref[...] = _dot(a_ref[...], b_ref[...]).astype(o_ref.dtype)


def _matmul(a, b, out_dtype, tm, tn, name):
    n, k = a.shape
    nc = b.shape[1]
    return pl.pallas_call(
        _mm_kernel,
        grid=(n // tm, nc // tn),
        in_specs=[pl.BlockSpec((tm, k), lambda i, j: (i, 0)), pl.BlockSpec((k, tn), lambda i, j: (0, j))],
        out_specs=pl.BlockSpec((tm, tn), lambda i, j: (i, j)),
        out_shape=jax.ShapeDtypeStruct((n, nc), out_dtype),
        compiler_params=_params(("parallel", "parallel")),
        name=name,
    )(a, b)


def _gdn_kernel(q_ref, k_ref, v_ref, z_ref, g_ref, cwq_ref, cwk_ref, cwv_ref, alog_ref, dtb_ref, nw_ref,
                o_ref, hist_ref, state_ref, *, heads, seq):
    hb = pl.program_id(1)
    incl, strict = _chunk_masks(SUPER)
    tril_bd = incl.astype(BF16)
    lane = lax.broadcasted_iota(jnp.int32, (1, HEAD), 1)
    neg_a = -jnp.exp(alog_ref[...])
    dtb = dtb_ref[...]
    nw = nw_ref[...]
    hist_ref[...] = jnp.zeros_like(hist_ref)
    state_ref[...] = jnp.zeros_like(state_ref)

    def conv_silu(x, hist, w):
        xe = jnp.concatenate([hist, x], axis=0)
        y = x * w[GDN_CONV - 1:GDN_CONV, :]
        for i in range(1, GDN_CONV):
            y = y + pltpu.roll(xe, i, 0)[8:, :] * w[GDN_CONV - 1 - i:GDN_CONV - i, :]
        return y * _sigmoid(y)

    def l2n(x):
        return x * lax.rsqrt(jnp.sum(x * x, axis=-1, keepdims=True) + EPS)

    def step(s, carry):
        r0 = pl.multiple_of(s * SUPER, SUPER)
        rows = pl.ds(r0, SUPER)
        gates = g_ref[rows, :]
        beta_all = _sigmoid(gates)
        g_all = neg_a * _softplus(gates + dtb)
        gc_all = _dot_exact_lhs(tril_bd, g_all)
        for hh in range(heads):
            cols = slice(hh * HEAD, (hh + 1) * HEAD)
            head = hb * heads + hh
            xq = q_ref[rows, cols].astype(F32)
            xk = k_ref[rows, cols].astype(F32)
            xv = v_ref[rows, cols].astype(F32)
            qn = l2n(conv_silu(xq, hist_ref[0, :, cols], cwq_ref[:, cols])) * (HEAD ** -0.5)
            kn = l2n(conv_silu(xk, hist_ref[1, :, cols], cwk_ref[:, cols]))
            vv = conv_silu(xv, hist_ref[2, :, cols], cwv_ref[:, cols])
            hist_ref[0, :, cols] = xq[SUPER - 8:, :]
            hist_ref[1, :, cols] = xk[SUPER - 8:, :]
            hist_ref[2, :, cols] = xv[SUPER - 8:, :]

            beta = jnp.sum(jnp.where(lane == LANE_GDN_B + head, beta_all, 0.0), axis=1, keepdims=True)
            gc = jnp.sum(jnp.where(lane == LANE_GDN_A + head, gc_all, 0.0), axis=1, keepdims=True)
            gcb = jnp.broadcast_to(gc, (SUPER, SUPER))
            gcl = gcb[:, :HEAD]
            decay = jnp.where(incl, jnp.exp(gcb - gcb.T), 0.0)
            kb = kn * beta
            knb = kn.astype(BF16)
            a_mat = jnp.where(strict, _dot_nt(kb.astype(BF16), knb) * decay, 0.0)
            qk = _dot_nt(qn.astype(BF16), knb) * decay
            rhs = jnp.concatenate([vv * beta, kb * jnp.exp(gcl)], axis=1)
            p = a_mat.astype(BF16)
            x = rhs - _dot(p, rhs.astype(BF16))
            for _ in range(5):
                p = _dot(p, p).astype(BF16)
                x = x + _dot(p, x.astype(BF16))
            u, w = x[:, :HEAD], x[:, HEAD:]
            qg = qn * jnp.exp(gcl)

            st = state_ref[hh]
            v_new, o_inter = [], []
            for c in range(SUPER // CHUNK):
                rs = slice(c * CHUNK, (c + 1) * CHUNK)
                g_last = gcl[(c + 1) * CHUNK - 1:(c + 1) * CHUNK, :]
                stb = st.astype(BF16)
                vn = u[rs] - _dot(w[rs].astype(BF16), stb)
                o_inter.append(_dot(qg[rs].astype(BF16), stb))
                ke = kn[rs] * jnp.exp(g_last - gcl[rs])
                st = st * jnp.exp(g_last) + _dot_tn(ke.astype(BF16), vn.astype(BF16))
                v_new.append(vn)
            state_ref[hh] = st
            o = jnp.concatenate(o_inter, axis=0) + _dot(qk.astype(BF16), jnp.concatenate(v_new, axis=0).astype(BF16))
            z = z_ref[rows, cols].astype(F32)
            o_ref[rows, cols] = (_rms(o, nw) * (z * _sigmoid(z))).astype(o_ref.dtype)
        return carry

    lax.fori_loop(0, seq // SUPER, step, 0)


def _gdn_mixer(proj, gates, conv_w, a_log, dt_bias, norm_w, heads=2):
    b, t, _ = proj.shape
    w = heads * HEAD
    nblk = H_GDN // heads
    seg = lambda k: (lambda bi, hb: (bi, 0, (OFF_GDN + k * GDN_W) // w + hb))
    cseg = lambda k: (lambda bi, hb: (0, (k * GDN_W) // w + hb))
    row = lambda v, off: jnp.zeros((1, HEAD), F32).at[0, off:off + v.shape[0]].set(v)
    act = lambda k: pl.BlockSpec((None, t, w), seg(k))
    cw = lambda k: pl.BlockSpec((GDN_CONV, w), cseg(k))
    vec = pl.BlockSpec((1, HEAD), lambda bi, hb: (0, 0))
    return pl.pallas_call(
        functools.partial(_gdn_kernel, heads=heads, seq=t),
        grid=(b, nblk),
        in_specs=[act(0), act(1), act(2), act(3), pl.BlockSpec((None, t, HEAD), lambda bi, hb: (bi, 0, 0)),
                  cw(0), cw(1), cw(2), vec, vec, vec],
        out_specs=pl.BlockSpec((None, t, w), lambda bi, hb: (bi, 0, hb)),
        out_shape=jax.ShapeDtypeStruct((b, t, GDN_W), BF16),
        scratch_shapes=[pltpu.VMEM((3, 8, w), F32), pltpu.VMEM((heads, HEAD, HEAD), F32)],
        compiler_params=_params(("parallel", "parallel")),
        name="gdn_mixer",
    )(proj, proj, proj, proj, gates, conv_w, conv_w, conv_w,
      row(a_log, LANE_GDN_A), row(dt_bias, LANE_GDN_A), norm_w.reshape(1, HEAD))


def _gla_kernel(q_ref, k_ref, v_ref, go_ref, g_ref, wg_ref, bg_ref, nw_ref, o_ref, state_ref, *, seq):
    incl, _ = _chunk_masks(SUPER)
    tril_bd = incl.astype(BF16)
    nw = nw_ref[...]
    wg = wg_ref[...]
    wg_hi = wg.astype(BF16)
    wg_lo = (wg - wg_hi.astype(F32)).astype(BF16)
    bg = bg_ref[...]
    state_ref[...] = jnp.zeros_like(state_ref)

    def step(s, carry):
        r0 = pl.multiple_of(s * SUPER, SUPER)
        rows = pl.ds(r0, SUPER)
        lr = g_ref[rows, :]
        lr_hi = lr.astype(BF16)
        lr_lo = (lr - lr_hi.astype(F32)).astype(BF16)
        logits = _dot(lr_hi, wg_hi) + _dot(lr_hi, wg_lo) + _dot(lr_lo, wg_hi) + bg
        gk_all = _log_sigmoid(logits) / GLA_NORMALIZER
        for hh in range(H_GLA):
            cols = slice(hh * HEAD, (hh + 1) * HEAD)
            bc = _dot_exact_lhs(tril_bd, gk_all[:, cols])
            q = q_ref[rows, cols].astype(F32) * (GLA_DK ** -0.5)
            k = k_ref[rows, cols].astype(F32)
            v = v_ref[rows, cols]
            q_dec = (q * jnp.exp(bc)).astype(BF16)
            k_inv = (k * jnp.exp(-bc)).astype(BF16)
            attn = jnp.where(incl, _dot_nt(q_dec, k_inv), 0.0)
            o_intra = _dot(attn.astype(BF16), v)
            st = state_ref[hh]
            o_inter = []
            for c in range(SUPER // CHUNK):
                rs = slice(c * CHUNK, (c + 1) * CHUNK)
                bc_last = bc[(c + 1) * CHUNK - 1:(c + 1) * CHUNK, :]
                o_inter.append(_dot_nt(q_dec[rs], st.astype(BF16)))
                k_end = (k[rs] * jnp.exp(bc_last - bc[rs])).astype(BF16)
                st = st * jnp.exp(bc_last) + _dot_tn(v[rs], k_end)
            state_ref[hh] = st
            o = o_intra + jnp.concatenate(o_inter, axis=0)
            go = go_ref[rows, cols].astype(F32)
            o_ref[rows, cols] = (_rms(o, nw) * (go * _sigmoid(go))).astype(o_ref.dtype)
        return carry

    lax.fori_loop(0, seq // SUPER, step, 0)


def _gla_mixer(proj, gates, w_gate, b_gate, norm_w):
    b, t, _ = proj.shape
    pad_heads = lambda m: jnp.pad(m.reshape(m.shape[0], H_GLA, GLA_DK),
                                  ((0, 0), (0, 0), (0, HEAD - GLA_DK))).reshape(m.shape[0], GLA_W)
    wg = jnp.zeros((HEAD, GLA_W), F32).at[LANE_GLA_LR:LANE_GLA_LR + GLA_RANK].set(pad_heads(w_gate))
    bg = pad_heads(b_gate.reshape(1, -1))
    act = lambda k: pl.BlockSpec((None, t, GLA_W), lambda bi: (bi, 0, OFF_GLA // GLA_W + k))
    const = lambda shape: pl.BlockSpec(shape, lambda bi: (0, 0))
    return pl.pallas_call(
        functools.partial(_gla_kernel, seq=t),
        grid=(b,),
        in_specs=[act(0), act(1), act(2), act(3), pl.BlockSpec((None, t, HEAD), lambda bi: (bi, 0, 0)),
                  const((HEAD, GLA_W)), const((1, GLA_W)), const((1, HEAD))],
        out_specs=pl.BlockSpec((None, t, GLA_W), lambda bi: (bi, 0, 0)),
        out_shape=jax.ShapeDtypeStruct((b, t, GLA_W), BF16),
        scratch_shapes=[pltpu.VMEM((H_GLA, HEAD, HEAD), F32)],
        compiler_params=_params(("parallel",)),
        name="gla_mixer",
    )(proj, proj, proj, proj, gates, wg, bg, norm_w.reshape(1, HEAD))


FOX_BLOCK = 512


def _fox_kernel(q_ref, k_ref, v_ref, g_ref, fb_ref, o_ref, c_ref, ct_ref, *, seq):
    blk = FOX_BLOCK
    row = lax.broadcasted_iota(jnp.int32, (SUPER, SUPER), 0)
    col = lax.broadcasted_iota(jnp.int32, (SUPER, SUPER), 1)
    tril = (row >= col).astype(BF16)
    fb = fb_ref[...]
    carry = jnp.zeros((1, HEAD), F32)
    for i in range(seq // SUPER):
        rs = slice(i * SUPER, (i + 1) * SUPER)
        cb = _dot_exact_lhs(tril, _log_sigmoid(g_ref[rs, :] + fb)) + carry
        c_ref[rs, :] = cb
        carry = cb[SUPER - 1:SUPER, :]
    for i in range(seq // blk):
        ct_ref[i] = c_ref[i * blk:(i + 1) * blk, :].T

    qrow = lax.broadcasted_iota(jnp.int32, (blk, blk), 0)
    kcol = lax.broadcasted_iota(jnp.int32, (blk, blk), 1)
    causal = qrow >= kcol
    scale = HEAD ** -0.5

    for hh in range(H_FOX):
        cols = slice(hh * HEAD, (hh + 1) * HEAD)
        lane = LANE_FOX_F + hh

        def q_step(qi, carry_q, cols=cols, lane=lane):
            q0 = pl.multiple_of(qi * blk, blk)
            q = q_ref[pl.ds(q0, blk), cols]
            cq = c_ref[pl.ds(q0, blk), :][:, lane:lane + 1]

            def tile(ki, m, l, acc, mask):
                k0 = pl.multiple_of(ki * blk, blk)
                k = k_ref[pl.ds(k0, blk), cols]
                v = v_ref[pl.ds(k0, blk), cols]
                ck = ct_ref[ki][lane:lane + 1, :]
                s = _dot_nt(q, k) * scale + (cq - ck)
                if mask:
                    s = jnp.where(causal, s, -jnp.inf)
                m_new = jnp.maximum(m, jnp.max(s, axis=1, keepdims=True))
                alpha = jnp.exp(m - m_new)
                p = jnp.exp(s - m_new)
                l = alpha * l + jnp.sum(p, axis=1, keepdims=True)
                acc = alpha * acc + _dot(p.astype(BF16), v)
                return m_new, l, acc

            init = (jnp.full((blk, 1), -jnp.inf, F32), jnp.zeros((blk, 1), F32), jnp.zeros((blk, HEAD), F32))
            m, l, acc = lax.fori_loop(0, qi, lambda ki, c: tile(ki, *c, mask=False), init)
            m, l, acc = tile(qi, m, l, acc, mask=True)
            o_ref[pl.ds(q0, blk), cols] = (acc / l).astype(o_ref.dtype)
            return carry_q

        lax.fori_loop(0, seq // blk, q_step, 0)


def _fox_mixer(proj, gates, f_bias):
    b, t, _ = proj.shape
    fb = jnp.zeros((1, HEAD), F32).at[0, LANE_FOX_F:LANE_FOX_F + H_FOX].set(f_bias)
    act = lambda k: pl.BlockSpec((None, t, FOX_W), lambda bi: (bi, 0, OFF_FOX // FOX_W + k))
    return pl.pallas_call(
        functools.partial(_fox_kernel, seq=t),
        grid=(b,),
        in_specs=[act(0), act(1), act(2), pl.BlockSpec((None, t, HEAD), lambda bi: (bi, 0, 0)),
                  pl.BlockSpec((1, HEAD), lambda bi: (0, 0))],
        out_specs=pl.BlockSpec((None, t, FOX_W), lambda bi: (bi, 0, 0)),
        out_shape=jax.ShapeDtypeStruct((b, t, FOX_W), BF16),
        scratch_shapes=[pltpu.VMEM((t, HEAD), F32), pltpu.VMEM((t // FOX_BLOCK, HEAD, FOX_BLOCK), F32)],
        compiler_params=_params(("parallel",)),
        name="fox_mixer",
    )(proj, proj, proj, gates, fb)


def _outproj_kernel(og_ref, ol_ref, of_ref, x_ref, w_ref, npost_ref, npre_ref, x1_ref, h_ref, cat_ref):
    cat_ref[:, :GDN_W] = og_ref[...]
    cat_ref[:, GDN_W:GDN_W + GLA_W] = ol_ref[...]
    cat_ref[:, GDN_W + GLA_W:] = of_ref[...]
    y = _dot(cat_ref[...], w_ref[...])
    x1 = x_ref[...] + _rms(y, npost_ref[...])
    x1_ref[...] = x1
    h_ref[...] = _rms(x1, npre_ref[...]).astype(h_ref.dtype)


def _outproj(o_gdn, o_gla, o_fox, x, w_out, n_post, n_pre_next, tm=512):
    n, d = x.shape
    rowblk = lambda wd: pl.BlockSpec((tm, wd), lambda i: (i, 0))
    const = lambda shape: pl.BlockSpec(shape, lambda i: (0, 0))
    return pl.pallas_call(
        _outproj_kernel,
        grid=(n // tm,),
        in_specs=[rowblk(GDN_W), rowblk(GLA_W), rowblk(FOX_W), rowblk(d), const((d, d)), const((1, d)), const((1, d))],
        out_specs=[rowblk(d), rowblk(d)],
        out_shape=[jax.ShapeDtypeStruct((n, d), F32), jax.ShapeDtypeStruct((n, d), BF16)],
        scratch_shapes=[pltpu.VMEM((tm, d), BF16)],
        compiler_params=_params(("parallel",)),
        name="out_proj",
    )(o_gdn, o_gla, o_fox, x, w_out, n_post.reshape(1, d), n_pre_next.reshape(1, d))


def _gelu_tanh(x):
    return 0.5 * x * (1.0 + jnp.tanh(math.sqrt(2.0 / math.pi) * (x + 0.044715 * (x * x * x))))


def _ffn_up_kernel(h_ref, wg_ref, wv_ref, cg_ref, cv_ref, bg_ref, bv_ref, o_ref):
    h = h_ref[...]
    seq = h.shape[0]
    first8 = lax.broadcasted_iota(jnp.int32, (8, 1), 0)

    def conv(u, c_ref, b_ref):
        c = c_ref[...]
        b = b_ref[...]
        y = u * c[2:3, :] + pltpu.roll(u, 1, 0) * c[1:2, :] + pltpu.roll(u, 2, 0) * c[0:1, :] + b
        u8 = u[:8, :]
        y8 = (u8 * c[2:3, :] + jnp.where(first8 >= 1, pltpu.roll(u8, 1, 0), 0.0) * c[1:2, :]
              + jnp.where(first8 >= 2, pltpu.roll(u8, 2, 0), 0.0) * c[0:1, :] + b)
        return y, y8

    g, g8 = conv(_dot(h, wg_ref[...]), cg_ref, bg_ref)
    v, v8 = conv(_dot(h, wv_ref[...]), cv_ref, bv_ref)
    o_ref[...] = (_gelu_tanh(g) * v).astype(o_ref.dtype)
    o_ref[:8, :] = (_gelu_tanh(g8) * v8).astype(o_ref.dtype)
    del seq


def _ffn_up(h, w_up, conv_w, conv_b, seq, tf=256):
    n, d = h.shape
    nf = D_FF // tf
    return pl.pallas_call(
        _ffn_up_kernel,
        grid=(n // seq, nf),
        in_specs=[pl.BlockSpec((seq, d), lambda i, j: (i, 0)),
                  pl.BlockSpec((d, tf), lambda i, j: (0, j)), pl.BlockSpec((d, tf), lambda i, j: (0, nf + j)),
                  pl.BlockSpec((3, tf), lambda i, j: (0, j)), pl.BlockSpec((3, tf), lambda i, j: (0, nf + j)),
                  pl.BlockSpec((1, tf), lambda i, j: (0, j)), pl.BlockSpec((1, tf), lambda i, j: (0, nf + j))],
        out_specs=pl.BlockSpec((seq, tf), lambda i, j: (i, j)),
        out_shape=jax.ShapeDtypeStruct((n, D_FF), BF16),
        compiler_params=_params(("parallel", "parallel")),
        name="ffn_up",
    )(h, w_up, w_up, conv_w, conv_w, conv_b.reshape(1, -1), conv_b.reshape(1, -1))


def _ffn_down_kernel(a_ref, w_ref, x_ref, npost_ref, nnext_ref, x2_ref, h_ref, acc_ref):
    kk = pl.program_id(1)

    @pl.when(kk == 0)
    def _():
        acc_ref[...] = jnp.zeros_like(acc_ref)

    acc_ref[...] += _dot(a_ref[...], w_ref[...])

    @pl.when(kk == pl.num_programs(1) - 1)
    def _():
        x2 = x_ref[...] + _rms(acc_ref[...], npost_ref[...])
        x2_ref[...] = x2
        h_ref[...] = _rms(x2, nnext_ref[...]).astype(h_ref.dtype)


def _ffn_down(a, w_down, x, n_post, n_next, tm=512, tk=1024):
    n, d = x.shape
    kdim = a.shape[1]
    const = pl.BlockSpec((1, d), lambda i, k: (0, 0))
    rowblk = pl.BlockSpec((tm, d), lambda i, k: (i, 0))
    return pl.pallas_call(
        _ffn_down_kernel,
        grid=(n // tm, kdim // tk),
        in_specs=[pl.BlockSpec((tm, tk), lambda i, k: (i, k)), pl.BlockSpec((tk, d), lambda i, k: (k, 0)),
                  rowblk, const, const],
        out_specs=[rowblk, rowblk],
        out_shape=[jax.ShapeDtypeStruct((n, d), F32), jax.ShapeDtypeStruct((n, d), BF16)],
        scratch_shapes=[pltpu.VMEM((tm, d), F32)],
        compiler_params=_params(("parallel", "arbitrary")),
        name="ffn_down",
    )(a, w_down, x, n_post.reshape(1, d), n_next.reshape(1, d))


def _pack_in_proj(w):
    d = w.shape[0]
    o = 0
    segs = {}
    for name, width in (("gdn", 4 * GDN_W), ("gdn_b", H_GDN), ("gdn_a", H_GDN), ("gla_q", H_GLA * GLA_DK),
                        ("gla_k", H_GLA * GLA_DK), ("gla_vg", 2 * GLA_W), ("gla_lr", GLA_RANK),
                        ("fox", 3 * FOX_W), ("fox_f", H_FOX)):
        segs[name] = w[:, o:o + width]
        o += width
    pad_heads = lambda m: jnp.pad(m.reshape(d, H_GLA, GLA_DK), ((0, 0), (0, 0), (0, HEAD - GLA_DK))).reshape(d, GLA_W)
    zeros = lambda n: jnp.zeros((d, n), w.dtype)
    main = jnp.concatenate([segs["gdn"], zeros(OFF_GLA - 4 * GDN_W), pad_heads(segs["gla_q"]), pad_heads(segs["gla_k"]),
                            segs["gla_vg"], segs["fox"]], axis=1)
    small = jnp.concatenate([segs["gdn_b"], segs["gdn_a"], segs["gla_lr"], segs["fox_f"],
                             zeros(HEAD - 2 * H_GDN - GLA_RANK - H_FOX)], axis=1)
    return main.astype(BF16), small.astype(BF16)


def kernel(x, w_in, conv_gdn, gdn_a_log, gdn_dt_bias, gdn_norm, gla_w_gate, gla_b_gate, gla_norm, fox_f_bias, w_out, norm_pre_mix, norm_post_mix, norm_pre_ffn, norm_post_ffn, w_up, conv_ffn, conv_ffn_bias, w_down):
    b, t, d = x.shape
    n = b * t
    depth = w_in.shape[0]
    xf = x.reshape(n, d)
    h = _rmsnorm(xf, norm_pre_mix[0])
    for i in range(depth):
        w_main, w_small = _pack_in_proj(w_in[i])
        proj = _matmul(h, w_main, BF16, tm=1024, tn=1280, name="in_proj").reshape(b, t, PROJ_W)
        gates = _matmul(h, w_small, F32, tm=1024, tn=HEAD, name="in_proj_gates").reshape(b, t, HEAD)
        o_gdn = _gdn_mixer(proj, gates, conv_gdn[i], gdn_a_log[i], gdn_dt_bias[i], gdn_norm[i])
        o_gla = _gla_mixer(proj, gates, gla_w_gate[i], gla_b_gate[i], gla_norm[i])
        o_fox = _fox_mixer(proj, gates, fox_f_bias[i])
        xf, h = _outproj(o_gdn.reshape(n, GDN_W), o_gla.reshape(n, GLA_W), o_fox.reshape(n, FOX_W), xf,
                         w_out[i].astype(BF16), norm_post_mix[i], norm_pre_ffn[i])
        a = _ffn_up(h, w_up[i].astype(BF16), conv_ffn[i], conv_ffn_bias[i], seq=t)
        n_next = norm_pre_mix[i + 1] if i + 1 < depth else norm_pre_mix[i]
        xf, h = _ffn_down(a, w_down[i].astype(BF16), xf, norm_post_ffn[i], n_next)
    return xf.reshape(b, t, d)
```

```python
import functools
import math

import jax
import jax.numpy as jnp
from jax import lax
from jax.experimental import pallas as pl
from jax.experimental.pallas import tpu as pltpu

F32 = jnp.float32
BF16 = jnp.bfloat16

D_MODEL = 2048
HEAD = 128
H_GDN, H_GLA, H_FOX = 6, 5, 5
GLA_DK = 64
GDN_CONV = 4
GLA_RANK = 16
GLA_NORMALIZER = 16.0
CHUNK = 64
SUPER = 256
D_FF = 4 * D_MODEL
EPS = 1e-6

GDN_W = H_GDN * HEAD
GLA_W = H_GLA * HEAD
FOX_W = H_FOX * HEAD
OFF_GDN = 0
OFF_GLA = 3200
OFF_FOX = 5760
PROJ_W = 7680
LANE_GDN_B, LANE_GDN_A, LANE_GLA_LR, LANE_FOX_F = 0, 6, 12, 28

V7X_VMEM_LIMIT = 56 * 1024 * 1024


def _params(sem, vmem=V7X_VMEM_LIMIT):
    return pltpu.CompilerParams(dimension_semantics=sem, vmem_limit_bytes=vmem)


def _dot(a, b):
    return jnp.dot(a, b, preferred_element_type=F32)


def _dot_nt(a, b):
    return lax.dot_general(a, b, (((1,), (1,)), ((), ())), preferred_element_type=F32)


def _dot_tn(a, b):
    return lax.dot_general(a, b, (((0,), (0,)), ((), ())), preferred_element_type=F32)


def _split3(x):
    x1 = x.astype(BF16)
    r = x - x1.astype(F32)
    x2 = r.astype(BF16)
    x3 = (r - x2.astype(F32)).astype(BF16)
    return x1, x2, x3


def _dot_exact_lhs(m, x):
    x1, x2, x3 = _split3(x)
    return _dot(m, x1) + _dot(m, x2) + _dot(m, x3)


def _sigmoid(x):
    return 1.0 / (1.0 + jnp.exp(-x))


def _softplus(x):
    return jnp.maximum(x, 0.0) + jnp.log1p(jnp.exp(-jnp.abs(x)))


def _log_sigmoid(x):
    return jnp.minimum(x, 0.0) - jnp.log1p(jnp.exp(-jnp.abs(x)))


def _rms(x, w):
    return x * lax.rsqrt(jnp.mean(x * x, axis=-1, keepdims=True) + EPS) * w


def _chunk_masks(n):
    row = lax.broadcasted_iota(jnp.int32, (n, n), 0)
    col = lax.broadcasted_iota(jnp.int32, (n, n), 1)
    same = (row // CHUNK) == (col // CHUNK)
    return same & (row >= col), same & (row > col)


def _rmsnorm_kernel(x_ref, w_ref, o_ref):
    o_ref[...] = _rms(x_ref[...], w_ref[...]).astype(o_ref.dtype)


def _rmsnorm(x, w, tm=512):
    n, d = x.shape
    return pl.pallas_call(
        _rmsnorm_kernel,
        grid=(n // tm,),
        in_specs=[pl.BlockSpec((tm, d), lambda i: (i, 0)), pl.BlockSpec((1, d), lambda i: (0, 0))],
        out_specs=pl.BlockSpec((tm, d), lambda i: (i, 0)),
        out_shape=jax.ShapeDtypeStruct((n, d), BF16),
        compiler_params=_params(("parallel",)),
        name="rmsnorm",
    )(x, w.reshape(1, d))


def _mm_nt_kernel(a_ref, bt_ref, o_ref):
    o_ref[...] = _dot_nt(a_ref[...], bt_ref[...]).astype(o_ref.dtype)


def _matmul_nt(a, bt, layer, out_dtype, tm, tn, name):
    n, k = a.shape
    nc = bt.shape[1]
    return pl.pallas_call(
        _mm_nt_kernel,
        grid=(n // tm, nc // tn),
        in_specs=[pl.BlockSpec((tm, k), lambda i, j: (i, 0)),
                  pl.BlockSpec((None, tn, k), lambda i, j: (layer, j, 0))],
        out_specs=pl.BlockSpec((tm, tn), lambda i, j: (i, j)),
        out_shape=jax.ShapeDtypeStruct((n, nc), out_dtype),
        compiler_params=_params(("parallel", "parallel")),
        name=name,
    )(a, bt)


def _gdn_kernel(q_ref, k_ref, v_ref, z_ref, g_ref, cw_ref, alog_ref, dtb_ref, nw_ref, o_ref, hist_ref, state_ref,
                *, seq):
    heads = range(H_GDN)
    cols = [slice(h * HEAD, (h + 1) * HEAD) for h in heads]
    incl, strict = _chunk_masks(SUPER)
    tril_bd = incl.astype(BF16)
    neg_a = -jnp.exp(alog_ref[...])
    dtb = dtb_ref[...]
    nw = nw_ref[...]
    hist_ref[...] = jnp.zeros_like(hist_ref)
    state_ref[...] = jnp.zeros_like(state_ref)

    def conv_silu(x, hist, w):
        xe = jnp.concatenate([hist, x], axis=0)
        y = x * w[GDN_CONV - 1:GDN_CONV, :]
        for i in range(1, GDN_CONV):
            y = y + pltpu.roll(xe, i, 0)[8:, :] * w[GDN_CONV - 1 - i:GDN_CONV - i, :]
        return y * _sigmoid(y)

    def l2n(x):
        return x * lax.rsqrt(jnp.sum(x * x, axis=-1, keepdims=True) + EPS)

    def conv_in(src_ref, k, rows):
        xs = [src_ref[rows, c].astype(F32) for c in cols]
        ys = [conv_silu(xs[h], hist_ref[k, :, cols[h]], cw_ref[:, k * GDN_W + h * HEAD:k * GDN_W + (h + 1) * HEAD])
              for h in heads]
        for h in heads:
            hist_ref[k, :, cols[h]] = xs[h][SUPER - 8:, :]
        return ys

    def step(s, carry):
        r0 = pl.multiple_of(s * SUPER, SUPER)
        rows = pl.ds(r0, SUPER)
        gates = g_ref[rows, :]
        beta_all = _sigmoid(gates)
        g_all = neg_a * _softplus(gates + dtb)
        gc_all = _dot_exact_lhs(tril_bd, g_all)
        qn = [l2n(y) * (HEAD ** -0.5) for y in conv_in(q_ref, 0, rows)]
        kn = [l2n(y) for y in conv_in(k_ref, 1, rows)]
        vv = conv_in(v_ref, 2, rows)

        beta = [beta_all[:, LANE_GDN_B + h:LANE_GDN_B + h + 1] for h in heads]
        gcb = [jnp.broadcast_to(gc_all[:, LANE_GDN_A + h:LANE_GDN_A + h + 1], (SUPER, SUPER)) for h in heads]
        gcl = [g[:, :HEAD] for g in gcb]
        decay = [jnp.where(incl, jnp.exp(g - g.T), 0.0) for g in gcb]
        kb = [kn[h] * beta[h] for h in heads]
        knb = [k.astype(BF16) for k in kn]
        sk = [_dot_nt(kb[h].astype(BF16), knb[h]) for h in heads]
        qk = [_dot_nt(qn[h].astype(BF16), knb[h]) for h in heads]
        a_mat = [jnp.where(strict, sk[h] * decay[h], 0.0) for h in heads]
        qk = [qk[h] * decay[h] for h in heads]
        rhs = [jnp.concatenate([vv[h] * beta[h], kb[h] * jnp.exp(gcl[h])], axis=1) for h in heads]
        p = [a.astype(BF16) for a in a_mat]
        x = [rhs[h] - _dot(p[h], rhs[h].astype(BF16)) for h in heads]
        for _ in range(5):
            p = [_dot(pi, pi).astype(BF16) for pi in p]
            x = [x[h] + _dot(p[h], x[h].astype(BF16)) for h in heads]
        u = [xi[:, :HEAD] for xi in x]
        w = [xi[:, HEAD:].astype(BF16) for xi in x]
        qg = [(qn[h] * jnp.exp(gcl[h])).astype(BF16) for h in heads]

        st = [state_ref[h] for h in heads]
        v_new = [[] for _ in heads]
        o_inter = [[] for _ in heads]
        for c in range(SUPER // CHUNK):
            rs = slice(c * CHUNK, (c + 1) * CHUNK)
            last = slice((c + 1) * CHUNK - 1, (c + 1) * CHUNK)
            stb = [t.astype(BF16) for t in st]
            vn = [u[h][rs] - _dot(w[h][rs], stb[h]) for h in heads]
            oi = [_dot(qg[h][rs], stb[h]) for h in heads]
            ke = [(kn[h][rs] * jnp.exp(gcl[h][last] - gcl[h][rs])).astype(BF16) for h in heads]
            st = [st[h] * jnp.exp(gcl[h][last]) + _dot_tn(ke[h], vn[h].astype(BF16)) for h in heads]
            for h in heads:
                v_new[h].append(vn[h])
                o_inter[h].append(oi[h])
        for h in heads:
            state_ref[h] = st[h]
        o = [jnp.concatenate(o_inter[h], axis=0)
             + _dot(qk[h].astype(BF16), jnp.concatenate(v_new[h], axis=0).astype(BF16)) for h in heads]
        for h in heads:
            z = z_ref[rows, cols[h]].astype(F32)
            o_ref[rows, cols[h]] = (_rms(o[h], nw) * (z * _sigmoid(z))).astype(o_ref.dtype)
        return carry

    lax.fori_loop(0, seq // SUPER, step, 0)


def _gdn_mixer(proj, gates, conv_w, a_log, dt_bias, norm_w):
    b, t, _ = proj.shape
    row = lambda v, off: jnp.zeros((1, HEAD), F32).at[0, off:off + v.shape[0]].set(v)
    act = lambda k: pl.BlockSpec((None, t, GDN_W), lambda bi: (bi, 0, OFF_GDN // GDN_W + k))
    vec = pl.BlockSpec((1, HEAD), lambda bi: (0, 0))
    return pl.pallas_call(
        functools.partial(_gdn_kernel, seq=t),
        grid=(b,),
        in_specs=[act(0), act(1), act(2), act(3), pl.BlockSpec((None, t, HEAD), lambda bi: (bi, 0, 0)),
                  pl.BlockSpec((GDN_CONV, 3 * GDN_W), lambda bi: (0, 0)), vec, vec, vec],
        out_specs=pl.BlockSpec((None, t, GDN_W), lambda bi: (bi, 0, 0)),
        out_shape=jax.ShapeDtypeStruct((b, t, GDN_W), BF16),
        scratch_shapes=[pltpu.VMEM((3, 8, GDN_W), F32), pltpu.VMEM((H_GDN, HEAD, HEAD), F32)],
        compiler_params=_params(("parallel",)),
        name="gdn_mixer",
    )(proj, proj, proj, proj, gates, conv_w, row(a_log, LANE_GDN_A), row(dt_bias, LANE_GDN_A),
      norm_w.reshape(1, HEAD))


def _gla_kernel(q_ref, k_ref, v_ref, go_ref, g_ref, wg_ref, bg_ref, nw_ref, o_ref, state_ref, *, seq):
    heads = range(H_GLA)
    cols = [slice(h * HEAD, (h + 1) * HEAD) for h in heads]
    incl, _ = _chunk_masks(SUPER)
    tril_bd = incl.astype(BF16)
    nw = nw_ref[...]
    wg = wg_ref[...]
    wg_hi = wg.astype(BF16)
    wg_lo = (wg - wg_hi.astype(F32)).astype(BF16)
    bg = bg_ref[...]
    state_ref[...] = jnp.zeros_like(state_ref)

    def step(s, carry):
        r0 = pl.multiple_of(s * SUPER, SUPER)
        rows = pl.ds(r0, SUPER)
        lr = g_ref[rows, :]
        lr_hi = lr.astype(BF16)
        lr_lo = (lr - lr_hi.astype(F32)).astype(BF16)
        logits = _dot(lr_hi, wg_hi) + _dot(lr_hi, wg_lo) + _dot(lr_lo, wg_hi) + bg
        gk_all = _log_sigmoid(logits) / GLA_NORMALIZER
        bc = [_dot_exact_lhs(tril_bd, gk_all[:, c]) for c in cols]
        k = [k_ref[rows, c].astype(F32) for c in cols]
        v = [v_ref[rows, c] for c in cols]
        q_dec = [(q_ref[rows, cols[h]].astype(F32) * (GLA_DK ** -0.5) * jnp.exp(bc[h])).astype(BF16) for h in heads]
        k_inv = [(k[h] * jnp.exp(-bc[h])).astype(BF16) for h in heads]
        attn = [jnp.where(incl, _dot_nt(q_dec[h], k_inv[h]), 0.0).astype(BF16) for h in heads]
        o_intra = [_dot(attn[h], v[h]) for h in heads]
        st = [state_ref[h] for h in heads]
        o_inter = [[] for _ in heads]
        for c in range(SUPER // CHUNK):
            rs = slice(c * CHUNK, (c + 1) * CHUNK)
            last = slice((c + 1) * CHUNK - 1, (c + 1) * CHUNK)
            oi = [_dot_nt(q_dec[h][rs], st[h].astype(BF16)) for h in heads]
            k_end = [(k[h][rs] * jnp.exp(bc[h][last] - bc[h][rs])).astype(BF16) for h in heads]
            st = [st[h] * jnp.exp(bc[h][last]) + _dot_tn(v[h][rs], k_end[h]) for h in heads]
            for h in heads:
                o_inter[h].append(oi[h])
        for h in heads:
            state_ref[h] = st[h]
            o = o_intra[h] + jnp.concatenate(o_inter[h], axis=0)
            go = go_ref[rows, cols[h]].astype(F32)
            o_ref[rows, cols[h]] = (_rms(o, nw) * (go * _sigmoid(go))).astype(o_ref.dtype)
        return carry

    lax.fori_loop(0, seq // SUPER, step, 0)


def _gla_mixer(proj, gates, w_gate, b_gate, norm_w):
    b, t, _ = proj.shape
    pad_heads = lambda m: jnp.pad(m.reshape(m.shape[0], H_GLA, GLA_DK),
                                  ((0, 0), (0, 0), (0, HEAD - GLA_DK))).reshape(m.shape[0], GLA_W)
    wg = jnp.zeros((HEAD, GLA_W), F32).at[LANE_GLA_LR:LANE_GLA_LR + GLA_RANK].set(pad_heads(w_gate))
    bg = pad_heads(b_gate.reshape(1, -1))
    act = lambda k: pl.BlockSpec((None, t, GLA_W), lambda bi: (bi, 0, OFF_GLA // GLA_W + k))
    const = lambda shape: pl.BlockSpec(shape, lambda bi: (0, 0))
    return pl.pallas_call(
        functools.partial(_gla_kernel, seq=t),
        grid=(b,),
        in_specs=[act(0), act(1), act(2), act(3), pl.BlockSpec((None, t, HEAD), lambda bi: (bi, 0, 0)),
                  const((HEAD, GLA_W)), const((1, GLA_W)), const((1, HEAD))],
        out_specs=pl.BlockSpec((None, t, GLA_W), lambda bi: (bi, 0, 0)),
        out_shape=jax.ShapeDtypeStruct((b, t, GLA_W), BF16),
        scratch_shapes=[pltpu.VMEM((H_GLA, HEAD, HEAD), F32)],
        compiler_params=_params(("parallel",)),
        name="gla_mixer",
    )(proj, proj, proj, proj, gates, wg, bg, norm_w.reshape(1, HEAD))


FOX_BLOCK = 512


def _fox_kernel(q_ref, k_ref, v_ref, g_ref, fb_ref, o_ref, c_ref, ct_ref, *, seq):
    blk = FOX_BLOCK
    heads = range(H_FOX)
    cols = [slice(h * HEAD, (h + 1) * HEAD) for h in heads]
    lanes = [slice(LANE_FOX_F + h, LANE_FOX_F + h + 1) for h in heads]
    row = lax.broadcasted_iota(jnp.int32, (SUPER, SUPER), 0)
    col = lax.broadcasted_iota(jnp.int32, (SUPER, SUPER), 1)
    tril = (row >= col).astype(BF16)
    fb = fb_ref[...]
    run = jnp.zeros((1, HEAD), F32)
    for i in range(seq // SUPER):
        rs = slice(i * SUPER, (i + 1) * SUPER)
        cb = _dot_exact_lhs(tril, _log_sigmoid(g_ref[rs, :] + fb)) + run
        c_ref[rs, :] = cb
        run = cb[SUPER - 1:SUPER, :]
    for i in range(seq // blk):
        ct_ref[i] = c_ref[i * blk:(i + 1) * blk, :].T

    causal = lax.broadcasted_iota(jnp.int32, (blk, blk), 0) >= lax.broadcasted_iota(jnp.int32, (blk, blk), 1)
    scale = HEAD ** -0.5

    def q_step(qi, carry_q):
        q0 = pl.multiple_of(qi * blk, blk)
        q = [q_ref[pl.ds(q0, blk), c] for c in cols]
        cq_all = c_ref[pl.ds(q0, blk), :]
        cq = [cq_all[:, ln] for ln in lanes]

        def tile(ki, state, mask):
            m, l, acc = state
            k0 = pl.multiple_of(ki * blk, blk)
            ck_all = ct_ref[ki]
            s = [_dot_nt(q[h], k_ref[pl.ds(k0, blk), cols[h]]) for h in heads]
            s = [s[h] * scale + (cq[h] - ck_all[lanes[h], :]) for h in heads]
            if mask:
                s = [jnp.where(causal, sh, -jnp.inf) for sh in s]
            m_new = [jnp.maximum(m[h], jnp.max(s[h], axis=1, keepdims=True)) for h in heads]
            alpha = [jnp.exp(m[h] - m_new[h]) for h in heads]
            p = [jnp.exp(s[h] - m_new[h]) for h in heads]
            l = [alpha[h] * l[h] + jnp.sum(p[h], axis=1, keepdims=True) for h in heads]
            pv = [_dot(p[h].astype(BF16), v_ref[pl.ds(k0, blk), cols[h]]) for h in heads]
            acc = [alpha[h] * acc[h] + pv[h] for h in heads]
            return m_new, l, acc

        init = ([jnp.full((blk, 1), -jnp.inf, F32) for _ in heads], [jnp.zeros((blk, 1), F32) for _ in heads],
                [jnp.zeros((blk, HEAD), F32) for _ in heads])
        state = lax.fori_loop(0, qi, lambda ki, st: tile(ki, st, mask=False), init)
        _, l, acc = tile(qi, state, mask=True)
        for h in heads:
            o_ref[pl.ds(q0, blk), cols[h]] = (acc[h] / l[h]).astype(o_ref.dtype)
        return carry_q

    lax.fori_loop(0, seq // blk, q_step, 0)


def _fox_mixer(proj, gates, f_bias):
    b, t, _ = proj.shape
    fb = jnp.zeros((1, HEAD), F32).at[0, LANE_FOX_F:LANE_FOX_F + H_FOX].set(f_bias)
    act = lambda k: pl.BlockSpec((None, t, FOX_W), lambda bi: (bi, 0, OFF_FOX // FOX_W + k))
    return pl.pallas_call(
        functools.partial(_fox_kernel, seq=t),
        grid=(b,),
        in_specs=[act(0), act(1), act(2), pl.BlockSpec((None, t, HEAD), lambda bi: (bi, 0, 0)),
                  pl.BlockSpec((1, HEAD), lambda bi: (0, 0))],
        out_specs=pl.BlockSpec((None, t, FOX_W), lambda bi: (bi, 0, 0)),
        out_shape=jax.ShapeDtypeStruct((b, t, FOX_W), BF16),
        scratch_shapes=[pltpu.VMEM((t, HEAD), F32), pltpu.VMEM((t // FOX_BLOCK, HEAD, FOX_BLOCK), F32)],
        compiler_params=_params(("parallel",)),
        name="fox_mixer",
    )(proj, proj, proj, gates, fb)


def _outproj_kernel(og_ref, ol_ref, of_ref, x_ref, w_ref, npost_ref, npre_ref, x1_ref, h_ref, cat_ref):
    cat_ref[:, :GDN_W] = og_ref[...]
    cat_ref[:, GDN_W:GDN_W + GLA_W] = ol_ref[...]
    cat_ref[:, GDN_W + GLA_W:] = of_ref[...]
    y = _dot(cat_ref[...], w_ref[...])
    x1 = x_ref[...] + _rms(y, npost_ref[...])
    x1_ref[...] = x1
    h_ref[...] = _rms(x1, npre_ref[...]).astype(h_ref.dtype)


def _outproj(o_gdn, o_gla, o_fox, x, w_out, layer, n_post, n_pre_next, tm=512):
    n, d = x.shape
    rowblk = lambda wd: pl.BlockSpec((tm, wd), lambda i: (i, 0))
    const = lambda shape: pl.BlockSpec(shape, lambda i: (0, 0))
    return pl.pallas_call(
        _outproj_kernel,
        grid=(n // tm,),
        in_specs=[rowblk(GDN_W), rowblk(GLA_W), rowblk(FOX_W), rowblk(d),
                  pl.BlockSpec((None, d, d), lambda i: (layer, 0, 0)), const((1, d)), const((1, d))],
        out_specs=[rowblk(d), rowblk(d)],
        out_shape=[jax.ShapeDtypeStruct((n, d), F32), jax.ShapeDtypeStruct((n, d), BF16)],
        scratch_shapes=[pltpu.VMEM((tm, d), BF16)],
        compiler_params=_params(("parallel",)),
        name="out_proj",
    )(o_gdn, o_gla, o_fox, x, w_out, n_post.reshape(1, d), n_pre_next.reshape(1, d))


def _gelu_tanh(x):
    return 0.5 * x * (1.0 + jnp.tanh(math.sqrt(2.0 / math.pi) * (x + 0.044715 * (x * x * x))))


def _ffn_up_kernel(h_ref, wg_ref, wv_ref, cg_ref, cv_ref, bg_ref, bv_ref, o_ref):
    h = h_ref[...]
    first8 = lax.broadcasted_iota(jnp.int32, (8, 1), 0)

    def conv(u, c_ref, b_ref):
        c = c_ref[...]
        b = b_ref[...]
        y = u * c[2:3, :] + pltpu.roll(u, 1, 0) * c[1:2, :] + pltpu.roll(u, 2, 0) * c[0:1, :] + b
        u8 = u[:8, :]
        y8 = (u8 * c[2:3, :] + jnp.where(first8 >= 1, pltpu.roll(u8, 1, 0), 0.0) * c[1:2, :]
              + jnp.where(first8 >= 2, pltpu.roll(u8, 2, 0), 0.0) * c[0:1, :] + b)
        return y, y8

    g, g8 = conv(_dot(h, wg_ref[...]), cg_ref, bg_ref)
    v, v8 = conv(_dot(h, wv_ref[...]), cv_ref, bv_ref)
    o_ref[...] = (_gelu_tanh(g) * v).astype(o_ref.dtype)
    o_ref[:8, :] = (_gelu_tanh(g8) * v8).astype(o_ref.dtype)


def _ffn_up(h, w_up, conv_w, conv_b, layer, seq, tf=256):
    n, d = h.shape
    nf = D_FF // tf
    conv_b = conv_b.reshape(conv_b.shape[0], 1, -1)
    wspec = lambda off: pl.BlockSpec((None, d, tf), lambda i, j: (layer, 0, off + j))
    cspec = lambda rows, off: pl.BlockSpec((None, rows, tf), lambda i, j: (layer, 0, off + j))
    return pl.pallas_call(
        _ffn_up_kernel,
        grid=(n // seq, nf),
        in_specs=[pl.BlockSpec((seq, d), lambda i, j: (i, 0)), wspec(0), wspec(nf),
                  cspec(3, 0), cspec(3, nf), cspec(1, 0), cspec(1, nf)],
        out_specs=pl.BlockSpec((seq, tf), lambda i, j: (i, j)),
        out_shape=jax.ShapeDtypeStruct((n, D_FF), BF16),
        compiler_params=_params(("parallel", "parallel")),
        name="ffn_up",
    )(h, w_up, w_up, conv_w, conv_w, conv_b, conv_b)


def _ffn_down_kernel(a_ref, w_ref, x_ref, npost_ref, nnext_ref, x2_ref, h_ref, acc_ref):
    kk = pl.program_id(1)

    @pl.when(kk == 0)
    def _():
        acc_ref[...] = jnp.zeros_like(acc_ref)

    acc_ref[...] += _dot(a_ref[...], w_ref[...])

    @pl.when(kk == pl.num_programs(1) - 1)
    def _():
        x2 = x_ref[...] + _rms(acc_ref[...], npost_ref[...])
        x2_ref[...] = x2
        h_ref[...] = _rms(x2, nnext_ref[...]).astype(h_ref.dtype)


def _ffn_down(a, w_down, layer, x, n_post, n_next, tm=512, tk=1024):
    n, d = x.shape
    kdim = a.shape[1]
    const = pl.BlockSpec((1, d), lambda i, k: (0, 0))
    rowblk = pl.BlockSpec((tm, d), lambda i, k: (i, 0))
    return pl.pallas_call(
        _ffn_down_kernel,
        grid=(n // tm, kdim // tk),
        in_specs=[pl.BlockSpec((tm, tk), lambda i, k: (i, k)),
                  pl.BlockSpec((None, tk, d), lambda i, k: (layer, k, 0)), rowblk, const, const],
        out_specs=[rowblk, rowblk],
        out_shape=[jax.ShapeDtypeStruct((n, d), F32), jax.ShapeDtypeStruct((n, d), BF16)],
        scratch_shapes=[pltpu.VMEM((tm, d), F32)],
        compiler_params=_params(("parallel", "arbitrary")),
        name="ffn_down",
    )(a, w_down, x, n_post.reshape(1, d), n_next.reshape(1, d))


def _pack_in_proj(w_in):
    wt = jnp.swapaxes(w_in, 1, 2)
    depth, _, d = wt.shape
    o = 0
    segs = {}
    for name, width in (("gdn", 4 * GDN_W), ("gdn_b", H_GDN), ("gdn_a", H_GDN), ("gla_q", H_GLA * GLA_DK),
                        ("gla_k", H_GLA * GLA_DK), ("gla_vg", 2 * GLA_W), ("gla_lr", GLA_RANK),
                        ("fox", 3 * FOX_W), ("fox_f", H_FOX)):
        segs[name] = wt[:, o:o + width]
        o += width
    pad_heads = lambda m: jnp.pad(m.reshape(depth, H_GLA, GLA_DK, d),
                                  ((0, 0), (0, 0), (0, HEAD - GLA_DK), (0, 0))).reshape(depth, GLA_W, d)
    zeros = lambda n: jnp.zeros((depth, n, d), wt.dtype)
    main = jnp.concatenate([segs["gdn"], zeros(OFF_GLA - 4 * GDN_W), pad_heads(segs["gla_q"]), pad_heads(segs["gla_k"]),
                            segs["gla_vg"], segs["fox"]], axis=1)
    small = jnp.concatenate([segs["gdn_b"], segs["gdn_a"], segs["gla_lr"], segs["fox_f"],
                             zeros(HEAD - 2 * H_GDN - GLA_RANK - H_FOX)], axis=1)
    return main.astype(BF16), small.astype(BF16)


def kernel(x, w_in, conv_gdn, gdn_a_log, gdn_dt_bias, gdn_norm, gla_w_gate, gla_b_gate, gla_norm, fox_f_bias, w_out, norm_pre_mix, norm_post_mix, norm_pre_ffn, norm_post_ffn, w_up, conv_ffn, conv_ffn_bias, w_down):
    b, t, d = x.shape
    n = b * t
    depth = w_in.shape[0]
    xf = x.reshape(n, d)
    w_main, w_small = _pack_in_proj(w_in)
    w_out_b, w_up_b, w_down_b = w_out.astype(BF16), w_up.astype(BF16), w_down.astype(BF16)
    h = _rmsnorm(xf, norm_pre_mix[0])
    for i in range(depth):
        proj = _matmul_nt(h, w_main, i, BF16, tm=1024, tn=1280, name="in_proj").reshape(b, t, PROJ_W)
        gates = _matmul_nt(h, w_small, i, F32, tm=1024, tn=HEAD, name="in_proj_gates").reshape(b, t, HEAD)
        o_gdn = _gdn_mixer(proj, gates, conv_gdn[i], gdn_a_log[i], gdn_dt_bias[i], gdn_norm[i])
        o_gla = _gla_mixer(proj, gates, gla_w_gate[i], gla_b_gate[i], gla_norm[i])
        o_fox = _fox_mixer(proj, gates, fox_f_bias[i])
        xf, h = _outproj(o_gdn.reshape(n, GDN_W), o_gla.reshape(n, GLA_W), o_fox.reshape(n, FOX_W), xf,
                         w_out_b, i, norm_post_mix[i], norm_pre_ffn[i])
        a = _ffn_up(h, w_up_b, conv_ffn, conv_ffn_bias, i, seq=t)
        n_next = norm_pre_mix[i + 1] if i + 1 < depth else norm_pre_mix[i]
        xf, h = _ffn_down(a, w_down_b, i, xf, norm_post_ffn[i], n_next)
    return xf.reshape(b, t, d)
```

```python
import functools
import math

import jax
import jax.numpy as jnp
from jax import lax
from jax.experimental import pallas as pl
from jax.experimental.pallas import tpu as pltpu

F32 = jnp.float32
BF16 = jnp.bfloat16

D_MODEL = 2048
HEAD = 128
H_GDN, H_GLA, H_FOX = 6, 5, 5
GLA_DK = 64
GDN_CONV = 4
GLA_RANK = 16
GLA_NORMALIZER = 16.0
CHUNK = 64
SUPER = 256
D_FF = 4 * D_MODEL
EPS = 1e-6

GDN_W = H_GDN * HEAD
GLA_W = H_GLA * HEAD
FOX_W = H_FOX * HEAD
OFF_GDN = 0
OFF_GLA = 3200
OFF_FOX = 5760
PROJ_W = 7680
LANE_GDN_B, LANE_GDN_A, LANE_GLA_LR, LANE_FOX_F = 0, 6, 12, 28

V7X_VMEM_LIMIT = 56 * 1024 * 1024


def _params(sem, vmem=V7X_VMEM_LIMIT, flags=None):
    return pltpu.CompilerParams(dimension_semantics=sem, vmem_limit_bytes=vmem, flags=flags)


def _dot(a, b):
    return jnp.dot(a, b, preferred_element_type=F32)


def _dot_nt(a, b):
    return lax.dot_general(a, b, (((1,), (1,)), ((), ())), preferred_element_type=F32)


def _dot_tn(a, b):
    return lax.dot_general(a, b, (((0,), (0,)), ((), ())), preferred_element_type=F32)


def _split3(x):
    x1 = x.astype(BF16)
    r = x - x1.astype(F32)
    x2 = r.astype(BF16)
    x3 = (r - x2.astype(F32)).astype(BF16)
    return x1, x2, x3


def _dot_exact_lhs(m, x):
    x1, x2, x3 = _split3(x)
    return _dot(m, x1) + _dot(m, x2) + _dot(m, x3)


def _sigmoid(x):
    return 1.0 / (1.0 + jnp.exp(-x))


def _softplus(x):
    return jnp.maximum(x, 0.0) + jnp.log1p(jnp.exp(-jnp.abs(x)))


def _log_sigmoid(x):
    return jnp.minimum(x, 0.0) - jnp.log1p(jnp.exp(-jnp.abs(x)))


def _rms(x, w):
    return x * lax.rsqrt(jnp.mean(x * x, axis=-1, keepdims=True) + EPS) * w


def _chunk_masks(n):
    row = lax.broadcasted_iota(jnp.int32, (n, n), 0)
    col = lax.broadcasted_iota(jnp.int32, (n, n), 1)
    same = (row // CHUNK) == (col // CHUNK)
    return same & (row >= col), same & (row > col)


def _rmsnorm_kernel(x_ref, w_ref, o_ref):
    o_ref[...] = _rms(x_ref[...], w_ref[...]).astype(o_ref.dtype)


def _rmsnorm(x, w, tm=512):
    n, d = x.shape
    return pl.pallas_call(
        _rmsnorm_kernel,
        grid=(n // tm,),
        in_specs=[pl.BlockSpec((tm, d), lambda i: (i, 0)), pl.BlockSpec((1, d), lambda i: (0, 0))],
        out_specs=pl.BlockSpec((tm, d), lambda i: (i, 0)),
        out_shape=jax.ShapeDtypeStruct((n, d), BF16),
        compiler_params=_params(("parallel",)),
        name="rmsnorm",
    )(x, w.reshape(1, d))


def _mm_nt_kernel(a_ref, bt_ref, o_ref):
    o_ref[...] = _dot_nt(a_ref[...], bt_ref[...]).astype(o_ref.dtype)


def _matmul_nt(a, bt, layer, out_dtype, tm, tn, name):
    n, k = a.shape
    nc = bt.shape[1]
    return pl.pallas_call(
        _mm_nt_kernel,
        grid=(n // tm, nc // tn),
        in_specs=[pl.BlockSpec((tm, k), lambda i, j: (i, 0)),
                  pl.BlockSpec((None, tn, k), lambda i, j: (layer, j, 0))],
        out_specs=pl.BlockSpec((tm, tn), lambda i, j: (i, j)),
        out_shape=jax.ShapeDtypeStruct((n, nc), out_dtype),
        compiler_params=_params(("parallel", "parallel")),
        name=name,
    )(a, bt)


def _gdn_kernel(q_ref, k_ref, v_ref, z_ref, g_ref, cw_ref, alog_ref, dtb_ref, nw_ref, o_ref, hist_ref, state_ref,
                *, seq):
    heads = range(H_GDN)
    cols = [slice(h * HEAD, (h + 1) * HEAD) for h in heads]
    incl, strict = _chunk_masks(SUPER)
    tril_bd = incl.astype(BF16)
    neg_a = -jnp.exp(alog_ref[...])
    dtb = dtb_ref[...]
    nw = nw_ref[...]
    hist_ref[...] = jnp.zeros_like(hist_ref)
    state_ref[...] = jnp.zeros_like(state_ref)

    def conv_silu(x, hist, w):
        xe = jnp.concatenate([hist, x], axis=0)
        y = x * w[GDN_CONV - 1:GDN_CONV, :]
        for i in range(1, GDN_CONV):
            y = y + pltpu.roll(xe, i, 0)[8:, :] * w[GDN_CONV - 1 - i:GDN_CONV - i, :]
        return y * _sigmoid(y)

    def l2n(x):
        return x * lax.rsqrt(jnp.sum(x * x, axis=-1, keepdims=True) + EPS)

    def conv_in(src_ref, k, rows):
        xs = [src_ref[rows, c].astype(F32) for c in cols]
        ys = [conv_silu(xs[h], hist_ref[k, :, cols[h]], cw_ref[:, k * GDN_W + h * HEAD:k * GDN_W + (h + 1) * HEAD])
              for h in heads]
        for h in heads:
            hist_ref[k, :, cols[h]] = xs[h][SUPER - 8:, :]
        return ys

    def step(s, carry):
        r0 = pl.multiple_of(s * SUPER, SUPER)
        rows = pl.ds(r0, SUPER)
        gates = g_ref[rows, :]
        beta_all = _sigmoid(gates)
        g_all = neg_a * _softplus(gates + dtb)
        gc_all = _dot_exact_lhs(tril_bd, g_all)
        qn = [l2n(y) * (HEAD ** -0.5) for y in conv_in(q_ref, 0, rows)]
        kn = [l2n(y) for y in conv_in(k_ref, 1, rows)]
        vv = conv_in(v_ref, 2, rows)

        beta = [beta_all[:, LANE_GDN_B + h:LANE_GDN_B + h + 1] for h in heads]
        gcb = [jnp.broadcast_to(gc_all[:, LANE_GDN_A + h:LANE_GDN_A + h + 1], (SUPER, SUPER)) for h in heads]
        gcl = [g[:, :HEAD] for g in gcb]
        decay = [jnp.where(incl, jnp.exp(g - g.T), 0.0) for g in gcb]
        kb = [kn[h] * beta[h] for h in heads]
        knb = [k.astype(BF16) for k in kn]
        sk = [_dot_nt(kb[h].astype(BF16), knb[h]) for h in heads]
        qk = [_dot_nt(qn[h].astype(BF16), knb[h]) for h in heads]
        a_mat = [jnp.where(strict, sk[h] * decay[h], 0.0) for h in heads]
        qk = [qk[h] * decay[h] for h in heads]
        rhs = [jnp.concatenate([vv[h] * beta[h], kb[h] * jnp.exp(gcl[h])], axis=1) for h in heads]
        p = [a.astype(BF16) for a in a_mat]
        x = [rhs[h] - _dot(p[h], rhs[h].astype(BF16)) for h in heads]
        for _ in range(5):
            p = [_dot(pi, pi).astype(BF16) for pi in p]
            x = [x[h] + _dot(p[h], x[h].astype(BF16)) for h in heads]
        u = [xi[:, :HEAD] for xi in x]
        w = [xi[:, HEAD:].astype(BF16) for xi in x]
        qg = [(qn[h] * jnp.exp(gcl[h])).astype(BF16) for h in heads]

        st = [state_ref[h] for h in heads]
        v_new = [[] for _ in heads]
        o_inter = [[] for _ in heads]
        for c in range(SUPER // CHUNK):
            rs = slice(c * CHUNK, (c + 1) * CHUNK)
            last = slice((c + 1) * CHUNK - 1, (c + 1) * CHUNK)
            stb = [t.astype(BF16) for t in st]
            vn = [u[h][rs] - _dot(w[h][rs], stb[h]) for h in heads]
            oi = [_dot(qg[h][rs], stb[h]) for h in heads]
            ke = [(kn[h][rs] * jnp.exp(gcl[h][last] - gcl[h][rs])).astype(BF16) for h in heads]
            st = [st[h] * jnp.exp(gcl[h][last]) + _dot_tn(ke[h], vn[h].astype(BF16)) for h in heads]
            for h in heads:
                v_new[h].append(vn[h])
                o_inter[h].append(oi[h])
        for h in heads:
            state_ref[h] = st[h]
        o = [jnp.concatenate(o_inter[h], axis=0)
             + _dot(qk[h].astype(BF16), jnp.concatenate(v_new[h], axis=0).astype(BF16)) for h in heads]
        for h in heads:
            z = z_ref[rows, cols[h]].astype(F32)
            o_ref[rows, cols[h]] = (_rms(o[h], nw) * (z * _sigmoid(z))).astype(o_ref.dtype)
        return carry

    lax.fori_loop(0, seq // SUPER, step, 0)


def _gdn_mixer(proj, gates, conv_w, a_log, dt_bias, norm_w):
    b, t, _ = proj.shape
    row = lambda v, off: jnp.zeros((1, HEAD), F32).at[0, off:off + v.shape[0]].set(v)
    act = lambda k: pl.BlockSpec((None, t, GDN_W), lambda bi: (bi, 0, OFF_GDN // GDN_W + k))
    vec = pl.BlockSpec((1, HEAD), lambda bi: (0, 0))
    return pl.pallas_call(
        functools.partial(_gdn_kernel, seq=t),
        grid=(b,),
        in_specs=[act(0), act(1), act(2), act(3), pl.BlockSpec((None, t, HEAD), lambda bi: (bi, 0, 0)),
                  pl.BlockSpec((GDN_CONV, 3 * GDN_W), lambda bi: (0, 0)), vec, vec, vec],
        out_specs=pl.BlockSpec((None, t, GDN_W), lambda bi: (bi, 0, 0)),
        out_shape=jax.ShapeDtypeStruct((b, t, GDN_W), BF16),
        scratch_shapes=[pltpu.VMEM((3, 8, GDN_W), F32), pltpu.VMEM((H_GDN, HEAD, HEAD), F32)],
        compiler_params=_params(("parallel",)),
        name="gdn_mixer",
    )(proj, proj, proj, proj, gates, conv_w, row(a_log, LANE_GDN_A), row(dt_bias, LANE_GDN_A),
      norm_w.reshape(1, HEAD))


def _gla_kernel(q_ref, k_ref, v_ref, go_ref, g_ref, wg_ref, bg_ref, nw_ref, o_ref, state_ref, *, seq):
    heads = range(H_GLA)
    cols = [slice(h * HEAD, (h + 1) * HEAD) for h in heads]
    incl, _ = _chunk_masks(SUPER)
    tril_bd = incl.astype(BF16)
    nw = nw_ref[...]
    wg = wg_ref[...]
    wg_hi = wg.astype(BF16)
    wg_lo = (wg - wg_hi.astype(F32)).astype(BF16)
    bg = bg_ref[...]
    state_ref[...] = jnp.zeros_like(state_ref)

    def step(s, carry):
        r0 = pl.multiple_of(s * SUPER, SUPER)
        rows = pl.ds(r0, SUPER)
        lr = g_ref[rows, :]
        lr_hi = lr.astype(BF16)
        lr_lo = (lr - lr_hi.astype(F32)).astype(BF16)
        logits = _dot(lr_hi, wg_hi) + _dot(lr_hi, wg_lo) + _dot(lr_lo, wg_hi) + bg
        gk_all = _log_sigmoid(logits) / GLA_NORMALIZER
        bc = [_dot_exact_lhs(tril_bd, gk_all[:, c]) for c in cols]
        k = [k_ref[rows, c].astype(F32) for c in cols]
        v = [v_ref[rows, c] for c in cols]
        q_dec = [(q_ref[rows, cols[h]].astype(F32) * (GLA_DK ** -0.5) * jnp.exp(bc[h])).astype(BF16) for h in heads]
        k_inv = [(k[h] * jnp.exp(-bc[h])).astype(BF16) for h in heads]
        attn = [jnp.where(incl, _dot_nt(q_dec[h], k_inv[h]), 0.0).astype(BF16) for h in heads]
        o_intra = [_dot(attn[h], v[h]) for h in heads]
        st = [state_ref[h] for h in heads]
        o_inter = [[] for _ in heads]
        for c in range(SUPER // CHUNK):
            rs = slice(c * CHUNK, (c + 1) * CHUNK)
            last = slice((c + 1) * CHUNK - 1, (c + 1) * CHUNK)
            oi = [_dot_nt(q_dec[h][rs], st[h].astype(BF16)) for h in heads]
            k_end = [(k[h][rs] * jnp.exp(bc[h][last] - bc[h][rs])).astype(BF16) for h in heads]
            st = [st[h] * jnp.exp(bc[h][last]) + _dot_tn(v[h][rs], k_end[h]) for h in heads]
            for h in heads:
                o_inter[h].append(oi[h])
        for h in heads:
            state_ref[h] = st[h]
            o = o_intra[h] + jnp.concatenate(o_inter[h], axis=0)
            go = go_ref[rows, cols[h]].astype(F32)
            o_ref[rows, cols[h]] = (_rms(o, nw) * (go * _sigmoid(go))).astype(o_ref.dtype)
        return carry

    lax.fori_loop(0, seq // SUPER, step, 0)


def _gla_mixer(proj, gates, w_gate, b_gate, norm_w):
    b, t, _ = proj.shape
    pad_heads = lambda m: jnp.pad(m.reshape(m.shape[0], H_GLA, GLA_DK),
                                  ((0, 0), (0, 0), (0, HEAD - GLA_DK))).reshape(m.shape[0], GLA_W)
    wg = jnp.zeros((HEAD, GLA_W), F32).at[LANE_GLA_LR:LANE_GLA_LR + GLA_RANK].set(pad_heads(w_gate))
    bg = pad_heads(b_gate.reshape(1, -1))
    act = lambda k: pl.BlockSpec((None, t, GLA_W), lambda bi: (bi, 0, OFF_GLA // GLA_W + k))
    const = lambda shape: pl.BlockSpec(shape, lambda bi: (0, 0))
    return pl.pallas_call(
        functools.partial(_gla_kernel, seq=t),
        grid=(b,),
        in_specs=[act(0), act(1), act(2), act(3), pl.BlockSpec((None, t, HEAD), lambda bi: (bi, 0, 0)),
                  const((HEAD, GLA_W)), const((1, GLA_W)), const((1, HEAD))],
        out_specs=pl.BlockSpec((None, t, GLA_W), lambda bi: (bi, 0, 0)),
        out_shape=jax.ShapeDtypeStruct((b, t, GLA_W), BF16),
        scratch_shapes=[pltpu.VMEM((H_GLA, HEAD, HEAD), F32)],
        compiler_params=_params(("parallel",)),
        name="gla_mixer",
    )(proj, proj, proj, proj, gates, wg, bg, norm_w.reshape(1, HEAD))


FOX_BLOCK = 512


def _fox_kernel(q_ref, k_ref, v_ref, g_ref, fb_ref, o_ref, c_ref, ct_ref, *, seq):
    blk = FOX_BLOCK
    heads = range(H_FOX)
    cols = [slice(h * HEAD, (h + 1) * HEAD) for h in heads]
    lanes = [slice(LANE_FOX_F + h, LANE_FOX_F + h + 1) for h in heads]
    row = lax.broadcasted_iota(jnp.int32, (SUPER, SUPER), 0)
    col = lax.broadcasted_iota(jnp.int32, (SUPER, SUPER), 1)
    tril = (row >= col).astype(BF16)
    fb = fb_ref[...]
    run = jnp.zeros((1, HEAD), F32)
    for i in range(seq // SUPER):
        rs = slice(i * SUPER, (i + 1) * SUPER)
        cb = _dot_exact_lhs(tril, _log_sigmoid(g_ref[rs, :] + fb)) + run
        c_ref[rs, :] = cb
        run = cb[SUPER - 1:SUPER, :]
    for i in range(seq // blk):
        ct_ref[i] = c_ref[i * blk:(i + 1) * blk, :].T

    causal = lax.broadcasted_iota(jnp.int32, (blk, blk), 0) >= lax.broadcasted_iota(jnp.int32, (blk, blk), 1)
    scale = HEAD ** -0.5

    def q_step(qi, carry_q):
        q0 = pl.multiple_of(qi * blk, blk)
        q = [q_ref[pl.ds(q0, blk), c] for c in cols]
        cq_all = c_ref[pl.ds(q0, blk), :]
        cq = [cq_all[:, ln] for ln in lanes]

        def tile(ki, state, mask):
            m, l, acc = state
            k0 = pl.multiple_of(ki * blk, blk)
            ck_all = ct_ref[ki]
            s = [_dot_nt(q[h], k_ref[pl.ds(k0, blk), cols[h]]) for h in heads]
            s = [s[h] * scale + (cq[h] - ck_all[lanes[h], :]) for h in heads]
            if mask:
                s = [jnp.where(causal, sh, -jnp.inf) for sh in s]
            m_new = [jnp.maximum(m[h], jnp.max(s[h], axis=1, keepdims=True)) for h in heads]
            alpha = [jnp.exp(m[h] - m_new[h]) for h in heads]
            p = [jnp.exp(s[h] - m_new[h]) for h in heads]
            l = [alpha[h] * l[h] + jnp.sum(p[h], axis=1, keepdims=True) for h in heads]
            pv = [_dot(p[h].astype(BF16), v_ref[pl.ds(k0, blk), cols[h]]) for h in heads]
            acc = [alpha[h] * acc[h] + pv[h] for h in heads]
            return m_new, l, acc

        init = ([jnp.full((blk, 1), -jnp.inf, F32) for _ in heads], [jnp.zeros((blk, 1), F32) for _ in heads],
                [jnp.zeros((blk, HEAD), F32) for _ in heads])
        state = lax.fori_loop(0, qi, lambda ki, st: tile(ki, st, mask=False), init)
        _, l, acc = tile(qi, state, mask=True)
        for h in heads:
            o_ref[pl.ds(q0, blk), cols[h]] = (acc[h] / l[h]).astype(o_ref.dtype)
        return carry_q

    lax.fori_loop(0, seq // blk, q_step, 0)


def _fox_mixer(proj, gates, f_bias):
    b, t, _ = proj.shape
    fb = jnp.zeros((1, HEAD), F32).at[0, LANE_FOX_F:LANE_FOX_F + H_FOX].set(f_bias)
    act = lambda k: pl.BlockSpec((None, t, FOX_W), lambda bi: (bi, 0, OFF_FOX // FOX_W + k))
    return pl.pallas_call(
        functools.partial(_fox_kernel, seq=t),
        grid=(b,),
        in_specs=[act(0), act(1), act(2), pl.BlockSpec((None, t, HEAD), lambda bi: (bi, 0, 0)),
                  pl.BlockSpec((1, HEAD), lambda bi: (0, 0))],
        out_specs=pl.BlockSpec((None, t, FOX_W), lambda bi: (bi, 0, 0)),
        out_shape=jax.ShapeDtypeStruct((b, t, FOX_W), BF16),
        scratch_shapes=[pltpu.VMEM((t, HEAD), F32), pltpu.VMEM((t // FOX_BLOCK, HEAD, FOX_BLOCK), F32)],
        compiler_params=_params(("parallel",)),
        name="fox_mixer",
    )(proj, proj, proj, gates, fb)


def _outproj_kernel(og_ref, ol_ref, of_ref, x_ref, w_ref, npost_ref, npre_ref, x1_ref, h_ref, cat_ref):
    cat_ref[:, :GDN_W] = og_ref[...]
    cat_ref[:, GDN_W:GDN_W + GLA_W] = ol_ref[...]
    cat_ref[:, GDN_W + GLA_W:] = of_ref[...]
    y = _dot(cat_ref[...], w_ref[...])
    x1 = x_ref[...] + _rms(y, npost_ref[...])
    x1_ref[...] = x1
    h_ref[...] = _rms(x1, npre_ref[...]).astype(h_ref.dtype)


def _outproj(o_gdn, o_gla, o_fox, x, w_out, layer, n_post, n_pre_next, tm=512):
    n, d = x.shape
    rowblk = lambda wd: pl.BlockSpec((tm, wd), lambda i: (i, 0))
    const = lambda shape: pl.BlockSpec(shape, lambda i: (0, 0))
    return pl.pallas_call(
        _outproj_kernel,
        grid=(n // tm,),
        in_specs=[rowblk(GDN_W), rowblk(GLA_W), rowblk(FOX_W), rowblk(d),
                  pl.BlockSpec((None, d, d), lambda i: (layer, 0, 0)), const((1, d)), const((1, d))],
        out_specs=[rowblk(d), rowblk(d)],
        out_shape=[jax.ShapeDtypeStruct((n, d), F32), jax.ShapeDtypeStruct((n, d), BF16)],
        scratch_shapes=[pltpu.VMEM((tm, d), BF16)],
        compiler_params=_params(("parallel",)),
        name="out_proj",
    )(o_gdn, o_gla, o_fox, x, w_out, n_post.reshape(1, d), n_pre_next.reshape(1, d))


def _gelu_tanh(x):
    k = 2.0 * math.sqrt(2.0 / math.pi)
    return x / (1.0 + jnp.exp(x * ((-k * 0.044715) * (x * x) - k)))


FFN_ROWS = 256


def _ffn_up_kernel(h_ref, wg_ref, wv_ref, cg_ref, cv_ref, bg_ref, bv_ref, o_ref):
    seq, tf = o_ref.shape
    wg = wg_ref[...]
    wv = wv_ref[...]

    def conv(u, tail, c_ref, b_ref):
        c = c_ref[...]
        ue = jnp.concatenate([tail, u], axis=0)
        return (u * c[2:3, :] + pltpu.roll(ue, 1, 0)[8:, :] * c[1:2, :] + pltpu.roll(ue, 2, 0)[8:, :] * c[0:1, :]
                + b_ref[...])

    tail_g = jnp.zeros((8, tf), F32)
    tail_v = jnp.zeros((8, tf), F32)
    for r in range(seq // FFN_ROWS):
        rows = slice(r * FFN_ROWS, (r + 1) * FFN_ROWS)
        h = h_ref[rows, :]
        ug = _dot(h, wg)
        uv = _dot(h, wv)
        o_ref[rows, :] = (_gelu_tanh(conv(ug, tail_g, cg_ref, bg_ref)) * conv(uv, tail_v, cv_ref, bv_ref)).astype(o_ref.dtype)
        tail_g, tail_v = ug[FFN_ROWS - 8:, :], uv[FFN_ROWS - 8:, :]


def _ffn_up(h, w_up, conv_w, conv_b, layer, seq, tf=1024):
    n, d = h.shape
    nf = D_FF // tf
    conv_b = conv_b.reshape(conv_b.shape[0], 1, -1)
    wspec = lambda off: pl.BlockSpec((None, d, tf), lambda i, j: (layer, 0, off + j))
    cspec = lambda rows, off: pl.BlockSpec((None, rows, tf), lambda i, j: (layer, 0, off + j))
    return pl.pallas_call(
        _ffn_up_kernel,
        grid=(n // seq, nf),
        in_specs=[pl.BlockSpec((seq, d), lambda i, j: (i, 0)), wspec(0), wspec(nf),
                  cspec(3, 0), cspec(3, nf), cspec(1, 0), cspec(1, nf)],
        out_specs=pl.BlockSpec((seq, tf), lambda i, j: (i, j)),
        out_shape=jax.ShapeDtypeStruct((n, D_FF), BF16),
        compiler_params=_params(("parallel", "parallel")),
        name="ffn_up",
    )(h, w_up, w_up, conv_w, conv_w, conv_b, conv_b)


def _ffn_down_kernel(a_ref, w_ref, x_ref, npost_ref, nnext_ref, x2_ref, h_ref):
    x2 = x_ref[...] + _rms(_dot(a_ref[...], w_ref[...]), npost_ref[...])
    x2_ref[...] = x2
    h_ref[...] = _rms(x2, nnext_ref[...]).astype(h_ref.dtype)


def _ffn_down(a, w_down, layer, x, n_post, n_next, tm=256):
    n, d = x.shape
    kdim = a.shape[1]
    const = pl.BlockSpec((1, d), lambda i: (0, 0))
    rowblk = pl.BlockSpec((tm, d), lambda i: (i, 0))
    return pl.pallas_call(
        _ffn_down_kernel,
        grid=(n // tm,),
        in_specs=[pl.BlockSpec((tm, kdim), lambda i: (i, 0)),
                  pl.BlockSpec((None, kdim, d), lambda i: (layer, 0, 0), pipeline_mode=pl.Buffered(1)),
                  rowblk, const, const],
        out_specs=[rowblk, rowblk],
        out_shape=[jax.ShapeDtypeStruct((n, d), F32), jax.ShapeDtypeStruct((n, d), BF16)],
        compiler_params=_params(("parallel",)),
        name="ffn_down",
    )(a, w_down, x, n_post.reshape(1, d), n_next.reshape(1, d))


def _pack_in_proj(w_in):
    wt = jnp.swapaxes(w_in, 1, 2)
    depth, _, d = wt.shape
    o = 0
    segs = {}
    for name, width in (("gdn", 4 * GDN_W), ("gdn_b", H_GDN), ("gdn_a", H_GDN), ("gla_q", H_GLA * GLA_DK),
                        ("gla_k", H_GLA * GLA_DK), ("gla_vg", 2 * GLA_W), ("gla_lr", GLA_RANK),
                        ("fox", 3 * FOX_W), ("fox_f", H_FOX)):
        segs[name] = wt[:, o:o + width]
        o += width
    pad_heads = lambda m: jnp.pad(m.reshape(depth, H_GLA, GLA_DK, d),
                                  ((0, 0), (0, 0), (0, HEAD - GLA_DK), (0, 0))).reshape(depth, GLA_W, d)
    zeros = lambda n: jnp.zeros((depth, n, d), wt.dtype)
    main = jnp.concatenate([segs["gdn"], zeros(OFF_GLA - 4 * GDN_W), pad_heads(segs["gla_q"]), pad_heads(segs["gla_k"]),
                            segs["gla_vg"], segs["fox"]], axis=1)
    small = jnp.concatenate([segs["gdn_b"], segs["gdn_a"], segs["gla_lr"], segs["fox_f"],
                             zeros(HEAD - 2 * H_GDN - GLA_RANK - H_FOX)], axis=1)
    return main.astype(BF16), small.astype(BF16)


def kernel(x, w_in, conv_gdn, gdn_a_log, gdn_dt_bias, gdn_norm, gla_w_gate, gla_b_gate, gla_norm, fox_f_bias, w_out, norm_pre_mix, norm_post_mix, norm_pre_ffn, norm_post_ffn, w_up, conv_ffn, conv_ffn_bias, w_down):
    b, t, d = x.shape
    n = b * t
    depth = w_in.shape[0]
    xf = x.reshape(n, d)
    w_main, w_small = _pack_in_proj(w_in)
    w_out_b, w_up_b, w_down_b = w_out.astype(BF16), w_up.astype(BF16), w_down.astype(BF16)
    h = _rmsnorm(xf, norm_pre_mix[0])
    for i in range(depth):
        proj = _matmul_nt(h, w_main, i, BF16, tm=1024, tn=1280, name="in_proj").reshape(b, t, PROJ_W)
        gates = _matmul_nt(h, w_small, i, F32, tm=1024, tn=HEAD, name="in_proj_gates").reshape(b, t, HEAD)
        o_gdn = _gdn_mixer(proj, gates, conv_gdn[i], gdn_a_log[i], gdn_dt_bias[i], gdn_norm[i])
        o_gla = _gla_mixer(proj, gates, gla_w_gate[i], gla_b_gate[i], gla_norm[i])
        o_fox = _fox_mixer(proj, gates, fox_f_bias[i])
        xf, h = _outproj(o_gdn.reshape(n, GDN_W), o_gla.reshape(n, GLA_W), o_fox.reshape(n, FOX_W), xf,
                         w_out_b, i, norm_post_mix[i], norm_pre_ffn[i])
        a = _ffn_up(h, w_up_b, conv_ffn, conv_ffn_bias, i, seq=t)
        n_next = norm_pre_mix[i + 1] if i + 1 < depth else norm_pre_mix[i]
        xf, h = _ffn_down(a, w_down_b, i, xf, norm_post_ffn[i], n_next)
    return xf.reshape(b, t, d)
```

```python
import functools
import math

import jax
import jax.numpy as jnp
from jax import lax
from jax.experimental import pallas as pl
from jax.experimental.pallas import tpu as pltpu

F32 = jnp.float32
BF16 = jnp.bfloat16

D_MODEL = 2048
HEAD = 128
H_GDN, H_GLA, H_FOX = 6, 5, 5
GLA_DK = 64
GDN_CONV = 4
GLA_RANK = 16
GLA_NORMALIZER = 16.0
CHUNK = 64
SUPER = 256
D_FF = 4 * D_MODEL
EPS = 1e-6
LOG2E = math.log2(math.e)

GDN_W = H_GDN * HEAD
GLA_W = H_GLA * HEAD
FOX_W = H_FOX * HEAD
QKV_W = 3 * GDN_W
OFF_GLA = 0
OFF_FOX = 2560
OFF_GDN_Z = 4608
REST_W = 5376
LANE_GDN_B, LANE_GDN_A, LANE_GLA_LR, LANE_FOX_F = 0, 6, 12, 28

V7X_VMEM_LIMIT = 56 * 1024 * 1024


def _params(sem, vmem=V7X_VMEM_LIMIT, flags=None):
    return pltpu.CompilerParams(dimension_semantics=sem, vmem_limit_bytes=vmem, flags=flags)


def _dot(a, b):
    return jnp.dot(a, b, preferred_element_type=F32)


def _dot_nt(a, b):
    return lax.dot_general(a, b, (((1,), (1,)), ((), ())), preferred_element_type=F32)


def _dot_tn(a, b):
    return lax.dot_general(a, b, (((0,), (0,)), ((), ())), preferred_element_type=F32)


def _split3(x):
    x1 = x.astype(BF16)
    r = x - x1.astype(F32)
    x2 = r.astype(BF16)
    x3 = (r - x2.astype(F32)).astype(BF16)
    return x1, x2, x3


def _dot_exact_lhs(m, x):
    x1, x2, x3 = _split3(x)
    return _dot(m, x1) + _dot(m, x2) + _dot(m, x3)


def _sigmoid(x):
    return 1.0 / (1.0 + jnp.exp(-x))


def _softplus(x):
    return jnp.maximum(x, 0.0) + jnp.log1p(jnp.exp(-jnp.abs(x)))


def _log_sigmoid(x):
    return jnp.minimum(x, 0.0) - jnp.log1p(jnp.exp(-jnp.abs(x)))


def _rms(x, w):
    return x * lax.rsqrt(jnp.mean(x * x, axis=-1, keepdims=True) + EPS) * w


def _chunk_masks(n):
    row = lax.broadcasted_iota(jnp.int32, (n, n), 0)
    col = lax.broadcasted_iota(jnp.int32, (n, n), 1)
    same = (row // CHUNK) == (col // CHUNK)
    return same & (row >= col), same & (row > col)


def _rmsnorm_kernel(x_ref, w_ref, o_ref):
    o_ref[...] = _rms(x_ref[...], w_ref[...]).astype(o_ref.dtype)


def _rmsnorm(x, w, tm=512):
    n, d = x.shape
    return pl.pallas_call(
        _rmsnorm_kernel,
        grid=(n // tm,),
        in_specs=[pl.BlockSpec((tm, d), lambda i: (i, 0)), pl.BlockSpec((1, d), lambda i: (0, 0))],
        out_specs=pl.BlockSpec((tm, d), lambda i: (i, 0)),
        out_shape=jax.ShapeDtypeStruct((n, d), BF16),
        compiler_params=_params(("parallel",)),
        name="rmsnorm",
    )(x, w.reshape(1, d))


IN_ROWS = 256


def _in_proj_conv_kernel(a_ref, bt_ref, cw_ref, o_ref, tail_ref, *, tiles_per_seq):
    w = bt_ref[...]
    cw = cw_ref[...]

    @pl.when(pl.program_id(0) % tiles_per_seq == 0)
    def _():
        tail_ref[...] = jnp.zeros_like(tail_ref)

    tail = tail_ref[...]
    for r in range(a_ref.shape[0] // IN_ROWS):
        rows = slice(r * IN_ROWS, (r + 1) * IN_ROWS)
        u = _dot_nt(a_ref[rows, :], w)
        ue = jnp.concatenate([tail, u], axis=0)
        y = u * cw[GDN_CONV - 1:GDN_CONV, :]
        for t in range(1, GDN_CONV):
            y = y + pltpu.roll(ue, t, 0)[8:, :] * cw[GDN_CONV - 1 - t:GDN_CONV - t, :]
        o_ref[rows, :] = (y * _sigmoid(y)).astype(o_ref.dtype)
        tail = u[IN_ROWS - 8:, :]
    tail_ref[...] = tail


def _in_proj_conv(a, bt, conv_w, layer, seq):
    n, k = a.shape
    nc = bt.shape[1]
    tm = min(1024, seq)
    return pl.pallas_call(
        functools.partial(_in_proj_conv_kernel, tiles_per_seq=seq // tm),
        grid=(n // tm,),
        in_specs=[pl.BlockSpec((tm, k), lambda i: (i, 0)),
                  pl.BlockSpec((None, nc, k), lambda i: (layer, 0, 0)),
                  pl.BlockSpec((None, GDN_CONV, nc), lambda i: (layer, 0, 0))],
        out_specs=pl.BlockSpec((tm, nc), lambda i: (i, 0)),
        out_shape=jax.ShapeDtypeStruct((n, nc), BF16),
        scratch_shapes=[pltpu.VMEM((8, nc), F32)],
        compiler_params=_params(("arbitrary",)),
        name="in_proj_qkv",
    )(a, bt, conv_w)


def _mm_nt_kernel(a_ref, bt_ref, o_ref):
    o_ref[...] = _dot_nt(a_ref[...], bt_ref[...]).astype(o_ref.dtype)


def _matmul_nt(a, bt, layer, out_dtype, tm, tn, name):
    n, k = a.shape
    nc = bt.shape[1]
    return pl.pallas_call(
        _mm_nt_kernel,
        grid=(n // tm, nc // tn),
        in_specs=[pl.BlockSpec((tm, k), lambda i, j: (i, 0)),
                  pl.BlockSpec((None, tn, k), lambda i, j: (layer, j, 0))],
        out_specs=pl.BlockSpec((tm, tn), lambda i, j: (i, j)),
        out_shape=jax.ShapeDtypeStruct((n, nc), out_dtype),
        compiler_params=_params(("parallel", "parallel")),
        name=name,
    )(a, bt)


def _gdn_kernel(q_ref, k_ref, v_ref, z_ref, g_ref, alog_ref, dtb_ref, nw_ref, o_ref, state_ref, *, seq):
    heads = range(H_GDN)
    cols = [slice(h * HEAD, (h + 1) * HEAD) for h in heads]
    incl, strict = _chunk_masks(SUPER)
    tril_bd = incl.astype(BF16)
    neg_a = -jnp.exp(alog_ref[...])
    dtb = dtb_ref[...]
    nw = nw_ref[...]
    state_ref[...] = jnp.zeros_like(state_ref)

    def l2n(x):
        return x * lax.rsqrt(jnp.sum(x * x, axis=-1, keepdims=True) + EPS)

    def step(s, carry):
        r0 = pl.multiple_of(s * SUPER, SUPER)
        rows = pl.ds(r0, SUPER)
        gates = g_ref[rows, :]
        beta_all = _sigmoid(gates)
        g_all = neg_a * _softplus(gates + dtb)
        gc_all = _dot_exact_lhs(tril_bd, g_all)
        qn = [l2n(q_ref[rows, c].astype(F32)) * (HEAD ** -0.5) for c in cols]
        kn = [l2n(k_ref[rows, c].astype(F32)) for c in cols]
        vv = [v_ref[rows, c].astype(F32) for c in cols]

        beta = [beta_all[:, LANE_GDN_B + h:LANE_GDN_B + h + 1] for h in heads]
        gcb = [jnp.broadcast_to(gc_all[:, LANE_GDN_A + h:LANE_GDN_A + h + 1], (SUPER, SUPER)) for h in heads]
        gcl = [g[:, :HEAD] for g in gcb]
        decay = [jnp.where(incl, jnp.exp(g - g.T), 0.0) for g in gcb]
        kb = [kn[h] * beta[h] for h in heads]
        knb = [k.astype(BF16) for k in kn]
        sk = [_dot_nt(kb[h].astype(BF16), knb[h]) for h in heads]
        qk = [_dot_nt(qn[h].astype(BF16), knb[h]) for h in heads]
        a_mat = [jnp.where(strict, sk[h] * decay[h], 0.0) for h in heads]
        qk = [qk[h] * decay[h] for h in heads]
        rhs = [jnp.concatenate([vv[h] * beta[h], kb[h] * jnp.exp(gcl[h])], axis=1) for h in heads]
        p = [a.astype(BF16) for a in a_mat]
        x = [rhs[h] - _dot(p[h], rhs[h].astype(BF16)) for h in heads]
        for _ in range(5):
            p = [_dot(pi, pi).astype(BF16) for pi in p]
            x = [x[h] + _dot(p[h], x[h].astype(BF16)) for h in heads]
        u = [xi[:, :HEAD] for xi in x]
        w = [xi[:, HEAD:].astype(BF16) for xi in x]
        qg = [(qn[h] * jnp.exp(gcl[h])).astype(BF16) for h in heads]

        st = [state_ref[h] for h in heads]
        v_new = [[] for _ in heads]
        o_inter = [[] for _ in heads]
        for c in range(SUPER // CHUNK):
            rs = slice(c * CHUNK, (c + 1) * CHUNK)
            last = slice((c + 1) * CHUNK - 1, (c + 1) * CHUNK)
            stb = [t.astype(BF16) for t in st]
            vn = [u[h][rs] - _dot(w[h][rs], stb[h]) for h in heads]
            oi = [_dot(qg[h][rs], stb[h]) for h in heads]
            ke = [(kn[h][rs] * jnp.exp(gcl[h][last] - gcl[h][rs])).astype(BF16) for h in heads]
            st = [st[h] * jnp.exp(gcl[h][last]) + _dot_tn(ke[h], vn[h].astype(BF16)) for h in heads]
            for h in heads:
                v_new[h].append(vn[h])
                o_inter[h].append(oi[h])
        for h in heads:
            state_ref[h] = st[h]
        o = [jnp.concatenate(o_inter[h], axis=0)
             + _dot(qk[h].astype(BF16), jnp.concatenate(v_new[h], axis=0).astype(BF16)) for h in heads]
        for h in heads:
            z = z_ref[rows, cols[h]].astype(F32)
            o_ref[rows, cols[h]] = (_rms(o[h], nw) * (z * _sigmoid(z))).astype(o_ref.dtype)
        return carry

    lax.fori_loop(0, seq // SUPER, step, 0)


def _gdn_mixer(qkv, rest, gates, a_log, dt_bias, norm_w):
    b, t, _ = qkv.shape
    row = lambda v, off: jnp.zeros((1, HEAD), F32).at[0, off:off + v.shape[0]].set(v)
    act = lambda k: pl.BlockSpec((None, t, GDN_W), lambda bi: (bi, 0, k))
    vec = pl.BlockSpec((1, HEAD), lambda bi: (0, 0))
    return pl.pallas_call(
        functools.partial(_gdn_kernel, seq=t),
        grid=(b,),
        in_specs=[act(0), act(1), act(2), act(OFF_GDN_Z // GDN_W), pl.BlockSpec((None, t, HEAD), lambda bi: (bi, 0, 0)),
                  vec, vec, vec],
        out_specs=pl.BlockSpec((None, t, GDN_W), lambda bi: (bi, 0, 0)),
        out_shape=jax.ShapeDtypeStruct((b, t, GDN_W), BF16),
        scratch_shapes=[pltpu.VMEM((H_GDN, HEAD, HEAD), F32)],
        compiler_params=_params(("parallel",)),
        name="gdn_mixer",
    )(qkv, qkv, qkv, rest, gates, row(a_log, LANE_GDN_A), row(dt_bias, LANE_GDN_A), norm_w.reshape(1, HEAD))


def _gla_kernel(q_ref, k_ref, v_ref, go_ref, g_ref, wg_ref, bg_ref, nw_ref, o_ref, state_ref, *, seq):
    heads = range(H_GLA)
    cols = [slice(h * HEAD, (h + 1) * HEAD) for h in heads]
    incl, _ = _chunk_masks(SUPER)
    tril_bd = incl.astype(BF16)
    nw = nw_ref[...]
    wg = wg_ref[...]
    wg_hi = wg.astype(BF16)
    wg_lo = (wg - wg_hi.astype(F32)).astype(BF16)
    bg = bg_ref[...]
    state_ref[...] = jnp.zeros_like(state_ref)

    def step(s, carry):
        r0 = pl.multiple_of(s * SUPER, SUPER)
        rows = pl.ds(r0, SUPER)
        lr = g_ref[rows, :]
        lr_hi = lr.astype(BF16)
        lr_lo = (lr - lr_hi.astype(F32)).astype(BF16)
        logits = _dot(lr_hi, wg_hi) + _dot(lr_hi, wg_lo) + _dot(lr_lo, wg_hi) + bg
        gk_all = _log_sigmoid(logits) / GLA_NORMALIZER
        bc = [_dot_exact_lhs(tril_bd, gk_all[:, c]) for c in cols]
        k = [k_ref[rows, c].astype(F32) for c in cols]
        v = [v_ref[rows, c] for c in cols]
        q_dec = [(q_ref[rows, cols[h]].astype(F32) * (GLA_DK ** -0.5) * jnp.exp(bc[h])).astype(BF16) for h in heads]
        k_inv = [(k[h] * jnp.exp(-bc[h])).astype(BF16) for h in heads]
        attn = [jnp.where(incl, _dot_nt(q_dec[h], k_inv[h]), 0.0).astype(BF16) for h in heads]
        o_intra = [_dot(attn[h], v[h]) for h in heads]
        st = [state_ref[h] for h in heads]
        o_inter = [[] for _ in heads]
        for c in range(SUPER // CHUNK):
            rs = slice(c * CHUNK, (c + 1) * CHUNK)
            last = slice((c + 1) * CHUNK - 1, (c + 1) * CHUNK)
            oi = [_dot_nt(q_dec[h][rs], st[h].astype(BF16)) for h in heads]
            k_end = [(k[h][rs] * jnp.exp(bc[h][last] - bc[h][rs])).astype(BF16) for h in heads]
            st = [st[h] * jnp.exp(bc[h][last]) + _dot_tn(v[h][rs], k_end[h]) for h in heads]
            for h in heads:
                o_inter[h].append(oi[h])
        for h in heads:
            state_ref[h] = st[h]
            o = o_intra[h] + jnp.concatenate(o_inter[h], axis=0)
            go = go_ref[rows, cols[h]].astype(F32)
            o_ref[rows, cols[h]] = (_rms(o, nw) * (go * _sigmoid(go))).astype(o_ref.dtype)
        return carry

    lax.fori_loop(0, seq // SUPER, step, 0)


def _gla_mixer(proj, gates, w_gate, b_gate, norm_w):
    b, t, _ = proj.shape
    pad_heads = lambda m: jnp.pad(m.reshape(m.shape[0], H_GLA, GLA_DK),
                                  ((0, 0), (0, 0), (0, HEAD - GLA_DK))).reshape(m.shape[0], GLA_W)
    wg = jnp.zeros((HEAD, GLA_W), F32).at[LANE_GLA_LR:LANE_GLA_LR + GLA_RANK].set(pad_heads(w_gate))
    bg = pad_heads(b_gate.reshape(1, -1))
    act = lambda k: pl.BlockSpec((None, t, GLA_W), lambda bi: (bi, 0, OFF_GLA // GLA_W + k))
    const = lambda shape: pl.BlockSpec(shape, lambda bi: (0, 0))
    return pl.pallas_call(
        functools.partial(_gla_kernel, seq=t),
        grid=(b,),
        in_specs=[act(0), act(1), act(2), act(3), pl.BlockSpec((None, t, HEAD), lambda bi: (bi, 0, 0)),
                  const((HEAD, GLA_W)), const((1, GLA_W)), const((1, HEAD))],
        out_specs=pl.BlockSpec((None, t, GLA_W), lambda bi: (bi, 0, 0)),
        out_shape=jax.ShapeDtypeStruct((b, t, GLA_W), BF16),
        scratch_shapes=[pltpu.VMEM((H_GLA, HEAD, HEAD), F32)],
        compiler_params=_params(("parallel",)),
        name="gla_mixer",
    )(proj, proj, proj, proj, gates, wg, bg, norm_w.reshape(1, HEAD))


FOX_SPAN = 512
FOX_ROWS = 256


def _fox_kernel(q_ref, k_ref, v_ref, g_ref, fb_ref, o_ref, c_ref, ct_ref, *, seq):
    heads = range(H_FOX)
    cols = [slice(h * HEAD, (h + 1) * HEAD) for h in heads]
    lanes = [slice(LANE_FOX_F + h, LANE_FOX_F + h + 1) for h in heads]
    row = lax.broadcasted_iota(jnp.int32, (SUPER, SUPER), 0)
    col = lax.broadcasted_iota(jnp.int32, (SUPER, SUPER), 1)
    tril = (row >= col).astype(BF16)
    fb = fb_ref[...]
    run = jnp.zeros((1, HEAD), F32)
    for i in range(seq // SUPER):
        rs = slice(i * SUPER, (i + 1) * SUPER)
        cb = _dot_exact_lhs(tril, _log_sigmoid(g_ref[rs, :] + fb)) + run
        c_ref[rs, :] = cb
        ct_ref[:, rs] = cb.T
        run = cb[SUPER - 1:SUPER, :]

    qrow = lax.broadcasted_iota(jnp.int32, (FOX_ROWS, FOX_SPAN), 0)
    kcol = lax.broadcasted_iota(jnp.int32, (FOX_ROWS, FOX_SPAN), 1)
    scale = HEAD ** -0.5

    for span in range(seq // FOX_SPAN):
        hi = (span + 1) * FOX_SPAN
        lo = hi - FOX_SPAN

        def q_step(sb, carry, hi=hi, lo=lo):
            r0 = pl.multiple_of(lo + sb * FOX_ROWS, FOX_ROWS)
            cq_all = c_ref[pl.ds(r0, FOX_ROWS), :] * LOG2E
            causal = qrow + sb * FOX_ROWS >= kcol
            s = [_dot_nt(q_ref[pl.ds(r0, FOX_ROWS), cols[h]], k_ref[0:hi, cols[h]]) for h in heads]
            s = [s[h] * (scale * LOG2E) + (cq_all[:, lanes[h]] - ct_ref[lanes[h], 0:hi] * LOG2E) for h in heads]
            tail = [jnp.where(causal, sh[:, lo:], -jnp.inf) for sh in s]
            s = [jnp.concatenate([s[h][:, :lo], tail[h]], axis=1) for h in heads] if lo else tail
            p = [jnp.exp2(sh - jnp.max(sh, axis=1, keepdims=True)) for sh in s]
            l = [jnp.sum(ph, axis=1, keepdims=True) for ph in p]
            pv = [_dot(p[h].astype(BF16), v_ref[0:hi, cols[h]]) for h in heads]
            for h in heads:
                o_ref[pl.ds(r0, FOX_ROWS), cols[h]] = (pv[h] / l[h]).astype(o_ref.dtype)
            return carry

        lax.fori_loop(0, FOX_SPAN // FOX_ROWS, q_step, 0)


def _fox_mixer(proj, gates, f_bias):
    b, t, _ = proj.shape
    fb = jnp.zeros((1, HEAD), F32).at[0, LANE_FOX_F:LANE_FOX_F + H_FOX].set(f_bias)
    act = lambda k: pl.BlockSpec((None, t, FOX_W), lambda bi: (bi, 0, OFF_FOX // FOX_W + k))
    return pl.pallas_call(
        functools.partial(_fox_kernel, seq=t),
        grid=(b,),
        in_specs=[act(0), act(1), act(2), pl.BlockSpec((None, t, HEAD), lambda bi: (bi, 0, 0)),
                  pl.BlockSpec((1, HEAD), lambda bi: (0, 0))],
        out_specs=pl.BlockSpec((None, t, FOX_W), lambda bi: (bi, 0, 0)),
        out_shape=jax.ShapeDtypeStruct((b, t, FOX_W), BF16),
        scratch_shapes=[pltpu.VMEM((t, HEAD), F32), pltpu.VMEM((HEAD, t), F32)],
        compiler_params=_params(("parallel",)),
        name="fox_mixer",
    )(proj, proj, proj, gates, fb)


OUT_ROWS = 256


def _outproj_kernel(og_ref, ol_ref, of_ref, x_ref, w_ref, npost_ref, npre_ref, x1_ref, h_ref, cat_ref):
    cat_ref[:, :GDN_W] = og_ref[...]
    cat_ref[:, GDN_W:GDN_W + GLA_W] = ol_ref[...]
    cat_ref[:, GDN_W + GLA_W:] = of_ref[...]
    w = w_ref[...]
    for r in range(cat_ref.shape[0] // OUT_ROWS):
        rows = slice(r * OUT_ROWS, (r + 1) * OUT_ROWS)
        x1 = x_ref[rows, :] + _rms(_dot(cat_ref[rows, :], w), npost_ref[...])
        x1_ref[rows, :] = x1
        h_ref[rows, :] = _rms(x1, npre_ref[...]).astype(h_ref.dtype)


def _outproj(o_gdn, o_gla, o_fox, x, w_out, layer, n_post, n_pre_next, tm=512):
    n, d = x.shape
    rowblk = lambda wd: pl.BlockSpec((tm, wd), lambda i: (i, 0))
    const = lambda shape: pl.BlockSpec(shape, lambda i: (0, 0))
    return pl.pallas_call(
        _outproj_kernel,
        grid=(n // tm,),
        in_specs=[rowblk(GDN_W), rowblk(GLA_W), rowblk(FOX_W), rowblk(d),
                  pl.BlockSpec((None, d, d), lambda i: (layer, 0, 0)), const((1, d)), const((1, d))],
        out_specs=[rowblk(d), rowblk(d)],
        out_shape=[jax.ShapeDtypeStruct((n, d), F32), jax.ShapeDtypeStruct((n, d), BF16)],
        scratch_shapes=[pltpu.VMEM((tm, d), BF16)],
        compiler_params=_params(("parallel",)),
        name="out_proj",
    )(o_gdn, o_gla, o_fox, x, w_out, n_post.reshape(1, d), n_pre_next.reshape(1, d))


def _gelu_tanh(x):
    k = 2.0 * math.sqrt(2.0 / math.pi)
    return x / (1.0 + jnp.exp(x * ((-k * 0.044715) * (x * x) - k)))


FFN_ROWS = 256


def _ffn_up_kernel(h_ref, wg_ref, wv_ref, cg_ref, cv_ref, bg_ref, bv_ref, o_ref):
    seq, tf = o_ref.shape
    wg = wg_ref[...]
    wv = wv_ref[...]

    def conv(u, tail, c_ref, b_ref):
        c = c_ref[...]
        ue = jnp.concatenate([tail, u], axis=0)
        return (u * c[2:3, :] + pltpu.roll(ue, 1, 0)[8:, :] * c[1:2, :] + pltpu.roll(ue, 2, 0)[8:, :] * c[0:1, :]
                + b_ref[...])

    tail_g = jnp.zeros((8, tf), F32)
    tail_v = jnp.zeros((8, tf), F32)
    for r in range(seq // FFN_ROWS):
        rows = slice(r * FFN_ROWS, (r + 1) * FFN_ROWS)
        h = h_ref[rows, :]
        ug = _dot(h, wg)
        uv = _dot(h, wv)
        o_ref[rows, :] = (_gelu_tanh(conv(ug, tail_g, cg_ref, bg_ref)) * conv(uv, tail_v, cv_ref, bv_ref)).astype(o_ref.dtype)
        tail_g, tail_v = ug[FFN_ROWS - 8:, :], uv[FFN_ROWS - 8:, :]


def _ffn_up(h, w_up, conv_w, conv_b, layer, seq, tf=1024):
    n, d = h.shape
    nf = D_FF // tf
    conv_b = conv_b.reshape(conv_b.shape[0], 1, -1)
    wspec = lambda off: pl.BlockSpec((None, d, tf), lambda i, j: (layer, 0, off + j))
    cspec = lambda rows, off: pl.BlockSpec((None, rows, tf), lambda i, j: (layer, 0, off + j))
    return pl.pallas_call(
        _ffn_up_kernel,
        grid=(n // seq, nf),
        in_specs=[pl.BlockSpec((seq, d), lambda i, j: (i, 0)), wspec(0), wspec(nf),
                  cspec(3, 0), cspec(3, nf), cspec(1, 0), cspec(1, nf)],
        out_specs=pl.BlockSpec((seq, tf), lambda i, j: (i, j)),
        out_shape=jax.ShapeDtypeStruct((n, D_FF), BF16),
        compiler_params=_params(("parallel", "parallel")),
        name="ffn_up",
    )(h, w_up, w_up, conv_w, conv_w, conv_b, conv_b)


def _ffn_down_kernel(a_ref, w_ref, x_ref, npost_ref, nnext_ref, x2_ref, h_ref):
    x2 = x_ref[...] + _rms(_dot(a_ref[...], w_ref[...]), npost_ref[...])
    x2_ref[...] = x2
    h_ref[...] = _rms(x2, nnext_ref[...]).astype(h_ref.dtype)


def _ffn_down(a, w_down, layer, x, n_post, n_next, tm=256):
    n, d = x.shape
    kdim = a.shape[1]
    const = pl.BlockSpec((1, d), lambda i: (0, 0))
    rowblk = pl.BlockSpec((tm, d), lambda i: (i, 0))
    return pl.pallas_call(
        _ffn_down_kernel,
        grid=(n // tm,),
        in_specs=[pl.BlockSpec((tm, kdim), lambda i: (i, 0)),
                  pl.BlockSpec((None, kdim, d), lambda i: (layer, 0, 0), pipeline_mode=pl.Buffered(1)),
                  rowblk, const, const],
        out_specs=[rowblk, rowblk],
        out_shape=[jax.ShapeDtypeStruct((n, d), F32), jax.ShapeDtypeStruct((n, d), BF16)],
        compiler_params=_params(("parallel",)),
        name="ffn_down",
    )(a, w_down, x, n_post.reshape(1, d), n_next.reshape(1, d))


def _pack_in_proj(w_in):
    wt = jnp.swapaxes(w_in, 1, 2)
    depth, _, d = wt.shape
    o = 0
    segs = {}
    for name, width in (("gdn", 4 * GDN_W), ("gdn_b", H_GDN), ("gdn_a", H_GDN), ("gla_q", H_GLA * GLA_DK),
                        ("gla_k", H_GLA * GLA_DK), ("gla_vg", 2 * GLA_W), ("gla_lr", GLA_RANK),
                        ("fox", 3 * FOX_W), ("fox_f", H_FOX)):
        segs[name] = wt[:, o:o + width]
        o += width
    pad_heads = lambda m: jnp.pad(m.reshape(depth, H_GLA, GLA_DK, d),
                                  ((0, 0), (0, 0), (0, HEAD - GLA_DK), (0, 0))).reshape(depth, GLA_W, d)
    zeros = lambda n: jnp.zeros((depth, n, d), wt.dtype)
    qkv = segs["gdn"][:, :QKV_W]
    rest = jnp.concatenate([pad_heads(segs["gla_q"]), pad_heads(segs["gla_k"]), segs["gla_vg"], segs["fox"],
                            zeros(OFF_GDN_Z - OFF_FOX - 3 * FOX_W), segs["gdn"][:, QKV_W:]], axis=1)
    small = jnp.concatenate([segs["gdn_b"], segs["gdn_a"], segs["gla_lr"], segs["fox_f"],
                             zeros(HEAD - 2 * H_GDN - GLA_RANK - H_FOX)], axis=1)
    return qkv.astype(BF16), rest.astype(BF16), small.astype(BF16)


def kernel(x, w_in, conv_gdn, gdn_a_log, gdn_dt_bias, gdn_norm, gla_w_gate, gla_b_gate, gla_norm, fox_f_bias, w_out, norm_pre_mix, norm_post_mix, norm_pre_ffn, norm_post_ffn, w_up, conv_ffn, conv_ffn_bias, w_down):
    b, t, d = x.shape
    n = b * t
    depth = w_in.shape[0]
    xf = x.reshape(n, d)
    w_qkv, w_rest, w_small = _pack_in_proj(w_in)
    w_out_b, w_up_b, w_down_b = w_out.astype(BF16), w_up.astype(BF16), w_down.astype(BF16)
    h = _rmsnorm(xf, norm_pre_mix[0])
    for i in range(depth):
        qkv = _in_proj_conv(h, w_qkv, conv_gdn, i, seq=t).reshape(b, t, QKV_W)
        rest = _matmul_nt(h, w_rest, i, BF16, tm=1024, tn=REST_W // 3, name="in_proj").reshape(b, t, REST_W)
        gates = _matmul_nt(h, w_small, i, F32, tm=1024, tn=HEAD, name="in_proj_gates").reshape(b, t, HEAD)
        o_gdn = _gdn_mixer(qkv, rest, gates, gdn_a_log[i], gdn_dt_bias[i], gdn_norm[i])
        o_gla = _gla_mixer(rest, gates, gla_w_gate[i], gla_b_gate[i], gla_norm[i])
        o_fox = _fox_mixer(rest, gates, fox_f_bias[i])
        xf, h = _outproj(o_gdn.reshape(n, GDN_W), o_gla.reshape(n, GLA_W), o_fox.reshape(n, FOX_W), xf,
                         w_out_b, i, norm_post_mix[i], norm_pre_ffn[i])
        a = _ffn_up(h, w_up_b, conv_ffn, conv_ffn_bias, i, seq=t)
        n_next = norm_pre_mix[i + 1] if i + 1 < depth else norm_pre_mix[i]
        xf, h = _ffn_down(a, w_down_b, i, xf, norm_post_ffn[i], n_next)
    return xf.reshape(b, t, d)
```

```python
import functools
import math

import jax
import jax.numpy as jnp
from jax import lax
from jax.experimental import pallas as pl
from jax.experimental.pallas import tpu as pltpu

F32 = jnp.float32
BF16 = jnp.bfloat16

D_MODEL = 2048
HEAD = 128
H_GDN, H_GLA, H_FOX = 6, 5, 5
GLA_DK = 64
GDN_CONV = 4
GLA_RANK = 16
GLA_NORMALIZER = 16.0
CHUNK = 64
SUPER = 256
D_FF = 4 * D_MODEL
EPS = 1e-6
LOG2E = math.log2(math.e)

GDN_W = H_GDN * HEAD
GLA_W = H_GLA * HEAD
FOX_W = H_FOX * HEAD
QKV_W = 3 * GDN_W
OFF_GLA = 0
OFF_FOX = 1920
OFF_GDN_Z = 3840
REST_W = 4608
LANE_GDN_B, LANE_GDN_A, LANE_GLA_LR, LANE_FOX_F = 0, 6, 12, 28

V7X_VMEM_LIMIT = 56 * 1024 * 1024


def _params(sem, vmem=V7X_VMEM_LIMIT, flags=None):
    return pltpu.CompilerParams(dimension_semantics=sem, vmem_limit_bytes=vmem, flags=flags)


def _dot(a, b):
    return jnp.dot(a, b, preferred_element_type=F32)


def _dot_nt(a, b):
    return lax.dot_general(a, b, (((1,), (1,)), ((), ())), preferred_element_type=F32)


def _dot_tn(a, b):
    return lax.dot_general(a, b, (((0,), (0,)), ((), ())), preferred_element_type=F32)


def _split3(x):
    x1 = x.astype(BF16)
    r = x - x1.astype(F32)
    x2 = r.astype(BF16)
    x3 = (r - x2.astype(F32)).astype(BF16)
    return x1, x2, x3


def _dot_exact_lhs(m, x):
    x1, x2, x3 = _split3(x)
    return _dot(m, x1) + _dot(m, x2) + _dot(m, x3)


def _sigmoid(x):
    return 1.0 / (1.0 + jnp.exp(-x))


def _softplus(x):
    return jnp.maximum(x, 0.0) + jnp.log1p(jnp.exp(-jnp.abs(x)))


def _log_sigmoid(x):
    return jnp.minimum(x, 0.0) - jnp.log1p(jnp.exp(-jnp.abs(x)))


def _rms(x, w):
    return x * lax.rsqrt(jnp.mean(x * x, axis=-1, keepdims=True) + EPS) * w


def _chunk_masks(n):
    row = lax.broadcasted_iota(jnp.int32, (n, n), 0)
    col = lax.broadcasted_iota(jnp.int32, (n, n), 1)
    same = (row // CHUNK) == (col // CHUNK)
    return same & (row >= col), same & (row > col)


def _rmsnorm_kernel(x_ref, w_ref, o_ref):
    o_ref[...] = _rms(x_ref[...], w_ref[...]).astype(o_ref.dtype)


def _rmsnorm(x, w, tm=512):
    n, d = x.shape
    return pl.pallas_call(
        _rmsnorm_kernel,
        grid=(n // tm,),
        in_specs=[pl.BlockSpec((tm, d), lambda i: (i, 0)), pl.BlockSpec((1, d), lambda i: (0, 0))],
        out_specs=pl.BlockSpec((tm, d), lambda i: (i, 0)),
        out_shape=jax.ShapeDtypeStruct((n, d), BF16),
        compiler_params=_params(("parallel",)),
        name="rmsnorm",
    )(x, w.reshape(1, d))


IN_ROWS = 256


def _in_proj_conv_kernel(a_ref, bt_ref, cw_ref, o_ref, tail_ref, *, tiles_per_seq):
    w = bt_ref[...]
    cw = cw_ref[...]

    @pl.when(pl.program_id(0) % tiles_per_seq == 0)
    def _():
        tail_ref[...] = jnp.zeros_like(tail_ref)

    tail = tail_ref[...]
    for r in range(a_ref.shape[0] // IN_ROWS):
        rows = slice(r * IN_ROWS, (r + 1) * IN_ROWS)
        u = _dot_nt(a_ref[rows, :], w)
        ue = jnp.concatenate([tail, u], axis=0)
        y = u * cw[GDN_CONV - 1:GDN_CONV, :]
        for t in range(1, GDN_CONV):
            y = y + pltpu.roll(ue, t, 0)[8:, :] * cw[GDN_CONV - 1 - t:GDN_CONV - t, :]
        o_ref[rows, :] = (y * _sigmoid(y)).astype(o_ref.dtype)
        tail = u[IN_ROWS - 8:, :]
    tail_ref[...] = tail


def _in_proj_conv(a, bt, conv_w, layer, seq):
    n, k = a.shape
    nc = bt.shape[1]
    tm = min(1024, seq)
    return pl.pallas_call(
        functools.partial(_in_proj_conv_kernel, tiles_per_seq=seq // tm),
        grid=(n // tm,),
        in_specs=[pl.BlockSpec((tm, k), lambda i: (i, 0)),
                  pl.BlockSpec((None, nc, k), lambda i: (layer, 0, 0)),
                  pl.BlockSpec((None, GDN_CONV, nc), lambda i: (layer, 0, 0))],
        out_specs=pl.BlockSpec((tm, nc), lambda i: (i, 0)),
        out_shape=jax.ShapeDtypeStruct((n, nc), BF16),
        scratch_shapes=[pltpu.VMEM((8, nc), F32)],
        compiler_params=_params(("arbitrary",)),
        name="in_proj_qkv",
    )(a, bt, conv_w)


def _mm_nt_kernel(a_ref, bt_ref, o_ref):
    o_ref[...] = _dot_nt(a_ref[...], bt_ref[...]).astype(o_ref.dtype)


def _matmul_nt(a, bt, layer, out_dtype, tm, tn, name):
    n, k = a.shape
    nc = bt.shape[1]
    return pl.pallas_call(
        _mm_nt_kernel,
        grid=(n // tm, nc // tn),
        in_specs=[pl.BlockSpec((tm, k), lambda i, j: (i, 0)),
                  pl.BlockSpec((None, tn, k), lambda i, j: (layer, j, 0))],
        out_specs=pl.BlockSpec((tm, tn), lambda i, j: (i, j)),
        out_shape=jax.ShapeDtypeStruct((n, nc), out_dtype),
        compiler_params=_params(("parallel", "parallel")),
        name=name,
    )(a, bt)


def _gdn_kernel(q_ref, k_ref, v_ref, z_ref, g_ref, alog_ref, dtb_ref, nw_ref, o_ref, state_ref, *, seq):
    heads = range(H_GDN)
    cols = [slice(h * HEAD, (h + 1) * HEAD) for h in heads]
    incl, strict = _chunk_masks(SUPER)
    tril_bd = incl.astype(BF16)
    neg_a = -jnp.exp(alog_ref[...])
    dtb = dtb_ref[...]
    nw = nw_ref[...]
    state_ref[...] = jnp.zeros_like(state_ref)

    def l2n(x):
        return x * lax.rsqrt(jnp.sum(x * x, axis=-1, keepdims=True) + EPS)

    def step(s, carry):
        r0 = pl.multiple_of(s * SUPER, SUPER)
        rows = pl.ds(r0, SUPER)
        gates = g_ref[rows, :]
        beta_all = _sigmoid(gates)
        g_all = neg_a * _softplus(gates + dtb)
        gc_all = _dot_exact_lhs(tril_bd, g_all)
        qn = [l2n(q_ref[rows, c].astype(F32)) * (HEAD ** -0.5) for c in cols]
        kn = [l2n(k_ref[rows, c].astype(F32)) for c in cols]
        vv = [v_ref[rows, c].astype(F32) for c in cols]

        beta = [beta_all[:, LANE_GDN_B + h:LANE_GDN_B + h + 1] for h in heads]
        gcb = [jnp.broadcast_to(gc_all[:, LANE_GDN_A + h:LANE_GDN_A + h + 1], (SUPER, SUPER)) for h in heads]
        gcl = [g[:, :HEAD] for g in gcb]
        decay = [jnp.where(incl, jnp.exp(g - g.T), 0.0) for g in gcb]
        kb = [kn[h] * beta[h] for h in heads]
        knb = [k.astype(BF16) for k in kn]
        sk = [_dot_nt(kb[h].astype(BF16), knb[h]) for h in heads]
        qk = [_dot_nt(qn[h].astype(BF16), knb[h]) for h in heads]
        a_mat = [jnp.where(strict, sk[h] * decay[h], 0.0) for h in heads]
        qk = [qk[h] * decay[h] for h in heads]
        rhs = [jnp.concatenate([vv[h] * beta[h], kb[h] * jnp.exp(gcl[h])], axis=1) for h in heads]
        p = [a.astype(BF16) for a in a_mat]
        x = [rhs[h] - _dot(p[h], rhs[h].astype(BF16)) for h in heads]
        for _ in range(5):
            p = [_dot(pi, pi).astype(BF16) for pi in p]
            x = [x[h] + _dot(p[h], x[h].astype(BF16)) for h in heads]
        u = [xi[:, :HEAD] for xi in x]
        w = [xi[:, HEAD:].astype(BF16) for xi in x]
        qg = [(qn[h] * jnp.exp(gcl[h])).astype(BF16) for h in heads]

        st = [state_ref[h] for h in heads]
        v_new = [[] for _ in heads]
        o_inter = [[] for _ in heads]
        for c in range(SUPER // CHUNK):
            rs = slice(c * CHUNK, (c + 1) * CHUNK)
            last = slice((c + 1) * CHUNK - 1, (c + 1) * CHUNK)
            stb = [t.astype(BF16) for t in st]
            vn = [u[h][rs] - _dot(w[h][rs], stb[h]) for h in heads]
            oi = [_dot(qg[h][rs], stb[h]) for h in heads]
            ke = [(kn[h][rs] * jnp.exp(gcl[h][last] - gcl[h][rs])).astype(BF16) for h in heads]
            st = [st[h] * jnp.exp(gcl[h][last]) + _dot_tn(ke[h], vn[h].astype(BF16)) for h in heads]
            for h in heads:
                v_new[h].append(vn[h])
                o_inter[h].append(oi[h])
        for h in heads:
            state_ref[h] = st[h]
        o = [jnp.concatenate(o_inter[h], axis=0)
             + _dot(qk[h].astype(BF16), jnp.concatenate(v_new[h], axis=0).astype(BF16)) for h in heads]
        for h in heads:
            z = z_ref[rows, cols[h]].astype(F32)
            o_ref[rows, cols[h]] = (_rms(o[h], nw) * (z * _sigmoid(z))).astype(o_ref.dtype)
        return carry

    lax.fori_loop(0, seq // SUPER, step, 0)


def _gdn_mixer(qkv, rest, gates, a_log, dt_bias, norm_w):
    b, t, _ = qkv.shape
    row = lambda v, off: jnp.zeros((1, HEAD), F32).at[0, off:off + v.shape[0]].set(v)
    act = lambda k: pl.BlockSpec((None, t, GDN_W), lambda bi: (bi, 0, k))
    vec = pl.BlockSpec((1, HEAD), lambda bi: (0, 0))
    return pl.pallas_call(
        functools.partial(_gdn_kernel, seq=t),
        grid=(b,),
        in_specs=[act(0), act(1), act(2), act(OFF_GDN_Z // GDN_W), pl.BlockSpec((None, t, HEAD), lambda bi: (bi, 0, 0)),
                  vec, vec, vec],
        out_specs=pl.BlockSpec((None, t, GDN_W), lambda bi: (bi, 0, 0)),
        out_shape=jax.ShapeDtypeStruct((b, t, GDN_W), BF16),
        scratch_shapes=[pltpu.VMEM((H_GDN, HEAD, HEAD), F32)],
        compiler_params=_params(("parallel",)),
        name="gdn_mixer",
    )(qkv, qkv, qkv, rest, gates, row(a_log, LANE_GDN_A), row(dt_bias, LANE_GDN_A), norm_w.reshape(1, HEAD))


def _gla_kernel(qk_ref, v_ref, go_ref, g_ref, wg_ref, bg_ref, nw_ref, o_ref, state_ref, *, seq):
    heads = range(H_GLA)
    cols = [slice(h * HEAD, (h + 1) * HEAD) for h in heads]
    incl, _ = _chunk_masks(SUPER)
    tril_bd = incl.astype(BF16)
    is_q = lax.broadcasted_iota(jnp.int32, (1, HEAD), 1) < GLA_DK
    sign = jnp.where(is_q, 1.0, -1.0)
    q_scale = jnp.where(is_q, GLA_DK ** -0.5, 1.0)
    nw = nw_ref[...]
    wg = wg_ref[...]
    wg_hi = wg.astype(BF16)
    wg_lo = (wg - wg_hi.astype(F32)).astype(BF16)
    bg = bg_ref[...]
    state_ref[...] = jnp.zeros_like(state_ref)

    def k_half(x):
        return jnp.where(is_q, pltpu.roll(x, GLA_DK, 1), 0.0).astype(BF16)

    def step(s, carry):
        r0 = pl.multiple_of(s * SUPER, SUPER)
        rows = pl.ds(r0, SUPER)
        lr = g_ref[rows, :]
        lr_hi = lr.astype(BF16)
        lr_lo = (lr - lr_hi.astype(F32)).astype(BF16)
        logits = _dot(lr_hi, wg_hi) + _dot(lr_hi, wg_lo) + _dot(lr_lo, wg_hi) + bg
        gk_all = _log_sigmoid(logits) / GLA_NORMALIZER
        bc = [_dot_exact_lhs(tril_bd, gk_all[:, c]) for c in cols]
        x = [qk_ref[rows, c].astype(F32) for c in cols]
        v = [v_ref[rows, c] for c in cols]
        xe = [x[h] * (jnp.exp(bc[h] * sign) * q_scale) for h in heads]
        q_dec = [jnp.where(is_q, xe[h], 0.0).astype(BF16) for h in heads]
        k_inv = [k_half(xe[h]) for h in heads]
        attn = [jnp.where(incl, _dot_nt(q_dec[h], k_inv[h]), 0.0).astype(BF16) for h in heads]
        o_intra = [_dot(attn[h], v[h]) for h in heads]
        st = [state_ref[h] for h in heads]
        o_inter = [[] for _ in heads]
        for c in range(SUPER // CHUNK):
            rs = slice(c * CHUNK, (c + 1) * CHUNK)
            last = slice((c + 1) * CHUNK - 1, (c + 1) * CHUNK)
            oi = [_dot_nt(q_dec[h][rs], st[h].astype(BF16)) for h in heads]
            k_end = [k_half(x[h][rs] * jnp.exp(bc[h][last] - bc[h][rs])) for h in heads]
            st = [st[h] * jnp.exp(bc[h][last]) + _dot_tn(v[h][rs], k_end[h]) for h in heads]
            for h in heads:
                o_inter[h].append(oi[h])
        for h in heads:
            state_ref[h] = st[h]
            o = o_intra[h] + jnp.concatenate(o_inter[h], axis=0)
            go = go_ref[rows, cols[h]].astype(F32)
            o_ref[rows, cols[h]] = (_rms(o, nw) * (go * _sigmoid(go))).astype(o_ref.dtype)
        return carry

    lax.fori_loop(0, seq // SUPER, step, 0)


def _gla_mixer(proj, gates, w_gate, b_gate, norm_w):
    b, t, _ = proj.shape
    twice = lambda m: jnp.tile(m.reshape(m.shape[0], H_GLA, 1, GLA_DK), (1, 1, 2, 1)).reshape(m.shape[0], GLA_W)
    wg = jnp.zeros((HEAD, GLA_W), F32).at[LANE_GLA_LR:LANE_GLA_LR + GLA_RANK].set(twice(w_gate))
    bg = twice(b_gate.reshape(1, -1))
    act = lambda k: pl.BlockSpec((None, t, GLA_W), lambda bi: (bi, 0, OFF_GLA // GLA_W + k))
    const = lambda shape: pl.BlockSpec(shape, lambda bi: (0, 0))
    return pl.pallas_call(
        functools.partial(_gla_kernel, seq=t),
        grid=(b,),
        in_specs=[act(0), act(1), act(2), pl.BlockSpec((None, t, HEAD), lambda bi: (bi, 0, 0)),
                  const((HEAD, GLA_W)), const((1, GLA_W)), const((1, HEAD))],
        out_specs=pl.BlockSpec((None, t, GLA_W), lambda bi: (bi, 0, 0)),
        out_shape=jax.ShapeDtypeStruct((b, t, GLA_W), BF16),
        scratch_shapes=[pltpu.VMEM((H_GLA, HEAD, HEAD), F32)],
        compiler_params=_params(("parallel",)),
        name="gla_mixer",
    )(proj, proj, proj, gates, wg, bg, norm_w.reshape(1, HEAD))


FOX_SPAN = 512
FOX_ROWS = 256


def _fox_kernel(q_ref, k_ref, v_ref, g_ref, fb_ref, o_ref, c_ref, ct_ref, vaug_ref, *, seq):
    heads = range(H_FOX)
    cols = [slice(h * HEAD, (h + 1) * HEAD) for h in heads]
    ones_col = (lax.broadcasted_iota(jnp.int32, (seq, HEAD), 1) == 0).astype(BF16)
    for h in heads:
        vaug_ref[:, 2 * h * HEAD:(2 * h + 1) * HEAD] = v_ref[:, cols[h]]
        vaug_ref[:, (2 * h + 1) * HEAD:(2 * h + 2) * HEAD] = ones_col
    lanes = [slice(LANE_FOX_F + h, LANE_FOX_F + h + 1) for h in heads]
    row = lax.broadcasted_iota(jnp.int32, (SUPER, SUPER), 0)
    col = lax.broadcasted_iota(jnp.int32, (SUPER, SUPER), 1)
    tril = (row >= col).astype(BF16)
    fb = fb_ref[...]
    run = jnp.zeros((1, HEAD), F32)
    for i in range(seq // SUPER):
        rs = slice(i * SUPER, (i + 1) * SUPER)
        cb = _dot_exact_lhs(tril, _log_sigmoid(g_ref[rs, :] + fb)) + run
        c_ref[rs, :] = cb
        ct_ref[:, rs] = cb.T
        run = cb[SUPER - 1:SUPER, :]

    qrow = lax.broadcasted_iota(jnp.int32, (FOX_ROWS, FOX_SPAN), 0)
    kcol = lax.broadcasted_iota(jnp.int32, (FOX_ROWS, FOX_SPAN), 1)
    scale = HEAD ** -0.5

    for span in range(seq // FOX_SPAN):
        hi = (span + 1) * FOX_SPAN
        lo = hi - FOX_SPAN

        def q_step(sb, carry, hi=hi, lo=lo):
            r0 = pl.multiple_of(lo + sb * FOX_ROWS, FOX_ROWS)
            cq_all = c_ref[pl.ds(r0, FOX_ROWS), :] * LOG2E
            causal = qrow + sb * FOX_ROWS >= kcol
            s = [_dot_nt(q_ref[pl.ds(r0, FOX_ROWS), cols[h]], k_ref[0:hi, cols[h]]) for h in heads]
            s = [s[h] * (scale * LOG2E) + (cq_all[:, lanes[h]] - ct_ref[lanes[h], 0:hi] * LOG2E) for h in heads]
            tail = [jnp.where(causal, sh[:, lo:], -jnp.inf) for sh in s]
            s = [jnp.concatenate([s[h][:, :lo], tail[h]], axis=1) for h in heads] if lo else tail
            p = [jnp.exp2(sh - jnp.max(sh, axis=1, keepdims=True)).astype(BF16) for sh in s]
            pv = [_dot(p[h], vaug_ref[0:hi, 2 * h * HEAD:(2 * h + 2) * HEAD]) for h in heads]
            for h in heads:
                o_ref[pl.ds(r0, FOX_ROWS), cols[h]] = (pv[h][:, :HEAD] / pv[h][:, HEAD:HEAD + 1]).astype(o_ref.dtype)
            return carry

        lax.fori_loop(0, FOX_SPAN // FOX_ROWS, q_step, 0)


def _fox_mixer(proj, gates, f_bias):
    b, t, _ = proj.shape
    fb = jnp.zeros((1, HEAD), F32).at[0, LANE_FOX_F:LANE_FOX_F + H_FOX].set(f_bias)
    act = lambda k: pl.BlockSpec((None, t, FOX_W), lambda bi: (bi, 0, OFF_FOX // FOX_W + k))
    return pl.pallas_call(
        functools.partial(_fox_kernel, seq=t),
        grid=(b,),
        in_specs=[act(0), act(1), act(2), pl.BlockSpec((None, t, HEAD), lambda bi: (bi, 0, 0)),
                  pl.BlockSpec((1, HEAD), lambda bi: (0, 0))],
        out_specs=pl.BlockSpec((None, t, FOX_W), lambda bi: (bi, 0, 0)),
        out_shape=jax.ShapeDtypeStruct((b, t, FOX_W), BF16),
        scratch_shapes=[pltpu.VMEM((t, HEAD), F32), pltpu.VMEM((HEAD, t), F32), pltpu.VMEM((t, 2 * FOX_W), BF16)],
        compiler_params=_params(("parallel",)),
        name="fox_mixer",
    )(proj, proj, proj, gates, fb)


OUT_ROWS = 256


def _outproj_kernel(og_ref, ol_ref, of_ref, x_ref, w_ref, npost_ref, npre_ref, x1_ref, h_ref, cat_ref):
    cat_ref[:, :GDN_W] = og_ref[...]
    cat_ref[:, GDN_W:GDN_W + GLA_W] = ol_ref[...]
    cat_ref[:, GDN_W + GLA_W:] = of_ref[...]
    w = w_ref[...]
    for r in range(cat_ref.shape[0] // OUT_ROWS):
        rows = slice(r * OUT_ROWS, (r + 1) * OUT_ROWS)
        x1 = x_ref[rows, :] + _rms(_dot(cat_ref[rows, :], w), npost_ref[...])
        x1_ref[rows, :] = x1
        h_ref[rows, :] = _rms(x1, npre_ref[...]).astype(h_ref.dtype)


def _outproj(o_gdn, o_gla, o_fox, x, w_out, layer, n_post, n_pre_next, tm=512):
    n, d = x.shape
    rowblk = lambda wd: pl.BlockSpec((tm, wd), lambda i: (i, 0))
    const = lambda shape: pl.BlockSpec(shape, lambda i: (0, 0))
    return pl.pallas_call(
        _outproj_kernel,
        grid=(n // tm,),
        in_specs=[rowblk(GDN_W), rowblk(GLA_W), rowblk(FOX_W), rowblk(d),
                  pl.BlockSpec((None, d, d), lambda i: (layer, 0, 0)), const((1, d)), const((1, d))],
        out_specs=[rowblk(d), rowblk(d)],
        out_shape=[jax.ShapeDtypeStruct((n, d), F32), jax.ShapeDtypeStruct((n, d), BF16)],
        scratch_shapes=[pltpu.VMEM((tm, d), BF16)],
        compiler_params=_params(("parallel",)),
        name="out_proj",
    )(o_gdn, o_gla, o_fox, x, w_out, n_post.reshape(1, d), n_pre_next.reshape(1, d))


def _gelu_tanh(x):
    k = 2.0 * math.sqrt(2.0 / math.pi)
    return x / (1.0 + jnp.exp(x * ((-k * 0.044715) * (x * x) - k)))


FFN_ROWS = 256


def _ffn_up_kernel(h_ref, wg_ref, wv_ref, cg_ref, cv_ref, bg_ref, bv_ref, o_ref):
    seq, tf = o_ref.shape
    wg = wg_ref[...]
    wv = wv_ref[...]

    def conv(u, tail, c_ref, b_ref):
        c = c_ref[...]
        ue = jnp.concatenate([tail, u], axis=0)
        return (u * c[2:3, :] + pltpu.roll(ue, 1, 0)[8:, :] * c[1:2, :] + pltpu.roll(ue, 2, 0)[8:, :] * c[0:1, :]
                + b_ref[...])

    tail_g = jnp.zeros((8, tf), F32)
    tail_v = jnp.zeros((8, tf), F32)
    for r in range(seq // FFN_ROWS):
        rows = slice(r * FFN_ROWS, (r + 1) * FFN_ROWS)
        h = h_ref[rows, :]
        ug = _dot(h, wg)
        uv = _dot(h, wv)
        o_ref[rows, :] = (_gelu_tanh(conv(ug, tail_g, cg_ref, bg_ref)) * conv(uv, tail_v, cv_ref, bv_ref)).astype(o_ref.dtype)
        tail_g, tail_v = ug[FFN_ROWS - 8:, :], uv[FFN_ROWS - 8:, :]


def _ffn_up(h, w_up, conv_w, conv_b, layer, seq, tf=1024):
    n, d = h.shape
    nf = D_FF // tf
    conv_b = conv_b.reshape(conv_b.shape[0], 1, -1)
    wspec = lambda off: pl.BlockSpec((None, d, tf), lambda i, j: (layer, 0, off + j))
    cspec = lambda rows, off: pl.BlockSpec((None, rows, tf), lambda i, j: (layer, 0, off + j))
    return pl.pallas_call(
        _ffn_up_kernel,
        grid=(n // seq, nf),
        in_specs=[pl.BlockSpec((seq, d), lambda i, j: (i, 0)), wspec(0), wspec(nf),
                  cspec(3, 0), cspec(3, nf), cspec(1, 0), cspec(1, nf)],
        out_specs=pl.BlockSpec((seq, tf), lambda i, j: (i, j)),
        out_shape=jax.ShapeDtypeStruct((n, D_FF), BF16),
        compiler_params=_params(("parallel", "parallel")),
        name="ffn_up",
    )(h, w_up, w_up, conv_w, conv_w, conv_b, conv_b)


def _ffn_down_kernel(a_ref, w_ref, x_ref, npost_ref, *rest):
    x2 = x_ref[...] + _rms(_dot(a_ref[...], w_ref[...]), npost_ref[...])
    if len(rest) == 1:
        rest[0][...] = x2
    else:
        nnext_ref, x2_ref, h_ref = rest
        x2_ref[...] = x2
        h_ref[...] = _rms(x2, nnext_ref[...]).astype(h_ref.dtype)


def _ffn_down(a, w_down, layer, x, n_post, n_next=None, tm=256):
    n, d = x.shape
    kdim = a.shape[1]
    const = pl.BlockSpec((1, d), lambda i: (0, 0))
    rowblk = pl.BlockSpec((tm, d), lambda i: (i, 0))
    in_specs = [pl.BlockSpec((tm, kdim), lambda i: (i, 0)),
                pl.BlockSpec((None, kdim, d), lambda i: (layer, 0, 0), pipeline_mode=pl.Buffered(1)), rowblk, const]
    args = [a, w_down, x, n_post.reshape(1, d)]
    x_out = jax.ShapeDtypeStruct((n, d), F32)
    if n_next is None:
        out_specs, out_shape = rowblk, x_out
    else:
        in_specs.append(const)
        args.append(n_next.reshape(1, d))
        out_specs, out_shape = [rowblk, rowblk], [x_out, jax.ShapeDtypeStruct((n, d), BF16)]
    return pl.pallas_call(
        _ffn_down_kernel,
        grid=(n // tm,),
        in_specs=in_specs,
        out_specs=out_specs,
        out_shape=out_shape,
        compiler_params=_params(("parallel",)),
        name="ffn_down",
    )(*args)


def _pack_in_proj(w_in):
    wt = jnp.swapaxes(w_in, 1, 2)
    depth, _, d = wt.shape
    o = 0
    segs = {}
    for name, width in (("gdn", 4 * GDN_W), ("gdn_b", H_GDN), ("gdn_a", H_GDN), ("gla_q", H_GLA * GLA_DK),
                        ("gla_k", H_GLA * GLA_DK), ("gla_vg", 2 * GLA_W), ("gla_lr", GLA_RANK),
                        ("fox", 3 * FOX_W), ("fox_f", H_FOX)):
        segs[name] = wt[:, o:o + width]
        o += width
    per_head = lambda m: m.reshape(depth, H_GLA, GLA_DK, d)
    gla_qk = jnp.concatenate([per_head(segs["gla_q"]), per_head(segs["gla_k"])], axis=2).reshape(depth, GLA_W, d)
    zeros = lambda n: jnp.zeros((depth, n, d), wt.dtype)
    qkv = segs["gdn"][:, :QKV_W]
    rest = jnp.concatenate([gla_qk, segs["gla_vg"], segs["fox"], segs["gdn"][:, QKV_W:]], axis=1)
    small = jnp.concatenate([segs["gdn_b"], segs["gdn_a"], segs["gla_lr"], segs["fox_f"],
                             zeros(HEAD - 2 * H_GDN - GLA_RANK - H_FOX)], axis=1)
    return qkv.astype(BF16), rest.astype(BF16), small.astype(BF16)


def kernel(x, w_in, conv_gdn, gdn_a_log, gdn_dt_bias, gdn_norm, gla_w_gate, gla_b_gate, gla_norm, fox_f_bias, w_out, norm_pre_mix, norm_post_mix, norm_pre_ffn, norm_post_ffn, w_up, conv_ffn, conv_ffn_bias, w_down):
    b, t, d = x.shape
    n = b * t
    depth = w_in.shape[0]
    xf = x.reshape(n, d)
    w_qkv, w_rest, w_small = _pack_in_proj(w_in)
    w_out_b, w_up_b, w_down_b = w_out.astype(BF16), w_up.astype(BF16), w_down.astype(BF16)
    h = _rmsnorm(xf, norm_pre_mix[0])
    for i in range(depth):
        qkv = _in_proj_conv(h, w_qkv, conv_gdn, i, seq=t).reshape(b, t, QKV_W)
        rest = _matmul_nt(h, w_rest, i, BF16, tm=1024, tn=REST_W // 3, name="in_proj").reshape(b, t, REST_W)
        gates = _matmul_nt(h, w_small, i, F32, tm=1024, tn=HEAD, name="in_proj_gates").reshape(b, t, HEAD)
        o_gdn = _gdn_mixer(qkv, rest, gates, gdn_a_log[i], gdn_dt_bias[i], gdn_norm[i])
        o_gla = _gla_mixer(rest, gates, gla_w_gate[i], gla_b_gate[i], gla_norm[i])
        o_fox = _fox_mixer(rest, gates, fox_f_bias[i])
        xf, h = _outproj(o_gdn.reshape(n, GDN_W), o_gla.reshape(n, GLA_W), o_fox.reshape(n, FOX_W), xf,
                         w_out_b, i, norm_post_mix[i], norm_pre_ffn[i])
        a = _ffn_up(h, w_up_b, conv_ffn, conv_ffn_bias, i, seq=t)
        if i + 1 < depth:
            xf, h = _ffn_down(a, w_down_b, i, xf, norm_post_ffn[i], norm_pre_mix[i + 1])
        else:
            xf = _ffn_down(a, w_down_b, i, xf, norm_post_ffn[i])
    return xf.reshape(b, t, d)
```

```python
import functools
import math

import jax
import jax.numpy as jnp
from jax import lax
from jax.experimental import pallas as pl
from jax.experimental.pallas import tpu as pltpu

F32 = jnp.float32
BF16 = jnp.bfloat16

D_MODEL = 2048
HEAD = 128
H_GDN, H_GLA, H_FOX = 6, 5, 5
GLA_DK = 64
GDN_CONV = 4
GLA_RANK = 16
GLA_NORMALIZER = 16.0
CHUNK = 64
SUPER = 256
D_FF = 4 * D_MODEL
EPS = 1e-6
LOG2E = math.log2(math.e)

GDN_W = H_GDN * HEAD
GLA_W = H_GLA * HEAD
FOX_W = H_FOX * HEAD
QKV_W = 3 * GDN_W
OFF_GLA = 0
OFF_FOX = 1920
OFF_GDN_Z = 3840
REST_W = 4608
LANE_GDN_B, LANE_GDN_A, LANE_GLA_LR, LANE_FOX_F = 0, 6, 12, 28

V7X_VMEM_LIMIT = 56 * 1024 * 1024
FFN_UP_VMEM = 60 * 1024 * 1024


def _params(sem, vmem=V7X_VMEM_LIMIT, flags=None):
    return pltpu.CompilerParams(dimension_semantics=sem, vmem_limit_bytes=vmem, flags=flags)


def _dot(a, b):
    return jnp.dot(a, b, preferred_element_type=F32)


def _dot_nt(a, b):
    return lax.dot_general(a, b, (((1,), (1,)), ((), ())), preferred_element_type=F32)


def _dot_tn(a, b):
    return lax.dot_general(a, b, (((0,), (0,)), ((), ())), preferred_element_type=F32)


def _split3(x):
    x1 = x.astype(BF16)
    r = x - x1.astype(F32)
    x2 = r.astype(BF16)
    x3 = (r - x2.astype(F32)).astype(BF16)
    return x1, x2, x3


def _dot_exact_lhs(m, x):
    x1, x2, x3 = _split3(x)
    return _dot(m, x1) + _dot(m, x2) + _dot(m, x3)


def _sigmoid(x):
    return 1.0 / (1.0 + jnp.exp(-x))


def _softplus(x):
    return jnp.maximum(x, 0.0) + jnp.log1p(jnp.exp(-jnp.abs(x)))


def _log_sigmoid(x):
    return jnp.minimum(x, 0.0) - jnp.log1p(jnp.exp(-jnp.abs(x)))


def _rms(x, w):
    return x * lax.rsqrt(jnp.mean(x * x, axis=-1, keepdims=True) + EPS) * w


def _chunk_masks(n):
    row = lax.broadcasted_iota(jnp.int32, (n, n), 0)
    col = lax.broadcasted_iota(jnp.int32, (n, n), 1)
    same = (row // CHUNK) == (col // CHUNK)
    return same & (row >= col), same & (row > col)


def _rmsnorm_kernel(x_ref, w_ref, o_ref):
    o_ref[...] = _rms(x_ref[...], w_ref[...]).astype(o_ref.dtype)


def _rmsnorm(x, w, tm=512):
    n, d = x.shape
    return pl.pallas_call(
        _rmsnorm_kernel,
        grid=(n // tm,),
        in_specs=[pl.BlockSpec((tm, d), lambda i: (i, 0)), pl.BlockSpec((1, d), lambda i: (0, 0))],
        out_specs=pl.BlockSpec((tm, d), lambda i: (i, 0)),
        out_shape=jax.ShapeDtypeStruct((n, d), BF16),
        compiler_params=_params(("parallel",)),
        name="rmsnorm",
    )(x, w.reshape(1, d))


IN_ROWS = 256


def _in_proj_conv_kernel(a_ref, bt_ref, cw_ref, o_ref, tail_ref, *, tiles_per_seq):
    w = bt_ref[...]
    cw = cw_ref[...]

    @pl.when(pl.program_id(0) % tiles_per_seq == 0)
    def _():
        tail_ref[...] = jnp.zeros_like(tail_ref)

    tail = tail_ref[...]
    for r in range(a_ref.shape[0] // IN_ROWS):
        rows = slice(r * IN_ROWS, (r + 1) * IN_ROWS)
        u = _dot_nt(a_ref[rows, :], w)
        ue = jnp.concatenate([tail, u], axis=0)
        y = u * cw[GDN_CONV - 1:GDN_CONV, :]
        for t in range(1, GDN_CONV):
            y = y + pltpu.roll(ue, t, 0)[8:, :] * cw[GDN_CONV - 1 - t:GDN_CONV - t, :]
        o_ref[rows, :] = (y * _sigmoid(y)).astype(o_ref.dtype)
        tail = u[IN_ROWS - 8:, :]
    tail_ref[...] = tail


def _in_proj_conv(a, bt, conv_w, layer, seq):
    n, k = a.shape
    nc = bt.shape[1]
    tm = min(1024, seq)
    return pl.pallas_call(
        functools.partial(_in_proj_conv_kernel, tiles_per_seq=seq // tm),
        grid=(n // tm,),
        in_specs=[pl.BlockSpec((tm, k), lambda i: (i, 0)),
                  pl.BlockSpec((None, nc, k), lambda i: (layer, 0, 0)),
                  pl.BlockSpec((None, GDN_CONV, nc), lambda i: (layer, 0, 0))],
        out_specs=pl.BlockSpec((tm, nc), lambda i: (i, 0)),
        out_shape=jax.ShapeDtypeStruct((n, nc), BF16),
        scratch_shapes=[pltpu.VMEM((8, nc), F32)],
        compiler_params=_params(("arbitrary",)),
        name="in_proj_qkv",
    )(a, bt, conv_w)


def _mm_nt_kernel(a_ref, bt_ref, o_ref):
    o_ref[...] = _dot_nt(a_ref[...], bt_ref[...]).astype(o_ref.dtype)


def _matmul_nt(a, bt, layer, out_dtype, tm, tn, name):
    n, k = a.shape
    nc = bt.shape[1]
    return pl.pallas_call(
        _mm_nt_kernel,
        grid=(n // tm, nc // tn),
        in_specs=[pl.BlockSpec((tm, k), lambda i, j: (i, 0)),
                  pl.BlockSpec((None, tn, k), lambda i, j: (layer, j, 0))],
        out_specs=pl.BlockSpec((tm, tn), lambda i, j: (i, j)),
        out_shape=jax.ShapeDtypeStruct((n, nc), out_dtype),
        compiler_params=_params(("parallel", "parallel")),
        name=name,
    )(a, bt)


def _gdn_kernel(q_ref, k_ref, v_ref, z_ref, g_ref, alog_ref, dtb_ref, nw_ref, o_ref, state_ref, *, seq):
    heads = range(H_GDN)
    cols = [slice(h * HEAD, (h + 1) * HEAD) for h in heads]
    incl, strict = _chunk_masks(SUPER)
    tril_bd = incl.astype(BF16)
    neg_a = -jnp.exp(alog_ref[...])
    dtb = dtb_ref[...]
    nw = nw_ref[...]
    state_ref[...] = jnp.zeros_like(state_ref)

    def l2n(x):
        return x * lax.rsqrt(jnp.sum(x * x, axis=-1, keepdims=True) + EPS)

    def step(s, carry):
        r0 = pl.multiple_of(s * SUPER, SUPER)
        rows = pl.ds(r0, SUPER)
        gates = g_ref[rows, :]
        beta_all = _sigmoid(gates)
        g_all = neg_a * _softplus(gates + dtb)
        gc_all = _dot_exact_lhs(tril_bd, g_all)
        qn = [l2n(q_ref[rows, c].astype(F32)) * (HEAD ** -0.5) for c in cols]
        kn = [l2n(k_ref[rows, c].astype(F32)) for c in cols]
        vv = [v_ref[rows, c].astype(F32) for c in cols]

        beta = [beta_all[:, LANE_GDN_B + h:LANE_GDN_B + h + 1] for h in heads]
        gcb = [jnp.broadcast_to(gc_all[:, LANE_GDN_A + h:LANE_GDN_A + h + 1], (SUPER, SUPER)) for h in heads]
        gcl = [g[:, :HEAD] for g in gcb]
        decay = [jnp.where(incl, jnp.exp(g - g.T), 0.0) for g in gcb]
        kb = [kn[h] * beta[h] for h in heads]
        knb = [k.astype(BF16) for k in kn]
        sk = [_dot_nt(kb[h].astype(BF16), knb[h]) for h in heads]
        qk = [_dot_nt(qn[h].astype(BF16), knb[h]) for h in heads]
        a_mat = [jnp.where(strict, sk[h] * decay[h], 0.0) for h in heads]
        qk = [qk[h] * decay[h] for h in heads]
        rhs = [jnp.concatenate([vv[h] * beta[h], kb[h] * jnp.exp(gcl[h])], axis=1) for h in heads]
        p = [a.astype(BF16) for a in a_mat]
        x = [rhs[h] - _dot(p[h], rhs[h].astype(BF16)) for h in heads]
        for _ in range(5):
            p = [_dot(pi, pi).astype(BF16) for pi in p]
            x = [x[h] + _dot(p[h], x[h].astype(BF16)) for h in heads]
        u = [xi[:, :HEAD] for xi in x]
        w = [xi[:, HEAD:].astype(BF16) for xi in x]
        qg = [(qn[h] * jnp.exp(gcl[h])).astype(BF16) for h in heads]

        st = [state_ref[h] for h in heads]
        v_new = [[] for _ in heads]
        o_inter = [[] for _ in heads]
        for c in range(SUPER // CHUNK):
            rs = slice(c * CHUNK, (c + 1) * CHUNK)
            last = slice((c + 1) * CHUNK - 1, (c + 1) * CHUNK)
            stb = [t.astype(BF16) for t in st]
            vn = [u[h][rs] - _dot(w[h][rs], stb[h]) for h in heads]
            oi = [_dot(qg[h][rs], stb[h]) for h in heads]
            ke = [(kn[h][rs] * jnp.exp(gcl[h][last] - gcl[h][rs])).astype(BF16) for h in heads]
            st = [st[h] * jnp.exp(gcl[h][last]) + _dot_tn(ke[h], vn[h].astype(BF16)) for h in heads]
            for h in heads:
                v_new[h].append(vn[h])
                o_inter[h].append(oi[h])
        for h in heads:
            state_ref[h] = st[h]
        o = [jnp.concatenate(o_inter[h], axis=0)
             + _dot(qk[h].astype(BF16), jnp.concatenate(v_new[h], axis=0).astype(BF16)) for h in heads]
        for h in heads:
            z = z_ref[rows, cols[h]].astype(F32)
            o_ref[rows, cols[h]] = (_rms(o[h], nw) * (z * _sigmoid(z))).astype(o_ref.dtype)
        return carry

    lax.fori_loop(0, seq // SUPER, step, 0, unroll=4)


def _gdn_mixer(qkv, rest, gates, a_log, dt_bias, norm_w):
    b, t, _ = qkv.shape
    row = lambda v, off: jnp.zeros((1, HEAD), F32).at[0, off:off + v.shape[0]].set(v)
    act = lambda k: pl.BlockSpec((None, t, GDN_W), lambda bi: (bi, 0, k))
    vec = pl.BlockSpec((1, HEAD), lambda bi: (0, 0))
    return pl.pallas_call(
        functools.partial(_gdn_kernel, seq=t),
        grid=(b,),
        in_specs=[act(0), act(1), act(2), act(OFF_GDN_Z // GDN_W), pl.BlockSpec((None, t, HEAD), lambda bi: (bi, 0, 0)),
                  vec, vec, vec],
        out_specs=pl.BlockSpec((None, t, GDN_W), lambda bi: (bi, 0, 0)),
        out_shape=jax.ShapeDtypeStruct((b, t, GDN_W), BF16),
        scratch_shapes=[pltpu.VMEM((H_GDN, HEAD, HEAD), F32)],
        compiler_params=_params(("parallel",)),
        name="gdn_mixer",
    )(qkv, qkv, qkv, rest, gates, row(a_log, LANE_GDN_A), row(dt_bias, LANE_GDN_A), norm_w.reshape(1, HEAD))


def _gla_kernel(qk_ref, v_ref, go_ref, g_ref, wg_ref, bg_ref, nw_ref, o_ref, state_ref, *, seq):
    heads = range(H_GLA)
    cols = [slice(h * HEAD, (h + 1) * HEAD) for h in heads]
    incl, _ = _chunk_masks(SUPER)
    tril_bd = incl.astype(BF16)
    is_q = lax.broadcasted_iota(jnp.int32, (1, HEAD), 1) < GLA_DK
    sign = jnp.where(is_q, 1.0, -1.0)
    q_scale = jnp.where(is_q, GLA_DK ** -0.5, 1.0)
    nw = nw_ref[...]
    wg = wg_ref[...]
    wg_hi = wg.astype(BF16)
    wg_lo = (wg - wg_hi.astype(F32)).astype(BF16)
    bg = bg_ref[...]
    state_ref[...] = jnp.zeros_like(state_ref)

    def k_half(x):
        return jnp.where(is_q, pltpu.roll(x, GLA_DK, 1), 0.0).astype(BF16)

    def step(s, carry):
        r0 = pl.multiple_of(s * SUPER, SUPER)
        rows = pl.ds(r0, SUPER)
        lr = g_ref[rows, :]
        lr_hi = lr.astype(BF16)
        lr_lo = (lr - lr_hi.astype(F32)).astype(BF16)
        logits = _dot(lr_hi, wg_hi) + _dot(lr_hi, wg_lo) + _dot(lr_lo, wg_hi) + bg
        gk_all = _log_sigmoid(logits) / GLA_NORMALIZER
        bc = [_dot_exact_lhs(tril_bd, gk_all[:, c]) for c in cols]
        x = [qk_ref[rows, c].astype(F32) for c in cols]
        v = [v_ref[rows, c] for c in cols]
        xe = [x[h] * (jnp.exp(bc[h] * sign) * q_scale) for h in heads]
        q_dec = [jnp.where(is_q, xe[h], 0.0).astype(BF16) for h in heads]
        k_inv = [k_half(xe[h]) for h in heads]
        attn = [jnp.where(incl, _dot_nt(q_dec[h], k_inv[h]), 0.0).astype(BF16) for h in heads]
        o_intra = [_dot(attn[h], v[h]) for h in heads]
        st = [state_ref[h] for h in heads]
        o_inter = [[] for _ in heads]
        for c in range(SUPER // CHUNK):
            rs = slice(c * CHUNK, (c + 1) * CHUNK)
            last = slice((c + 1) * CHUNK - 1, (c + 1) * CHUNK)
            oi = [_dot_nt(q_dec[h][rs], st[h].astype(BF16)) for h in heads]
            k_end = [k_half(x[h][rs] * jnp.exp(bc[h][last] - bc[h][rs])) for h in heads]
            st = [st[h] * jnp.exp(bc[h][last]) + _dot_tn(v[h][rs], k_end[h]) for h in heads]
            for h in heads:
                o_inter[h].append(oi[h])
        for h in heads:
            state_ref[h] = st[h]
            o = o_intra[h] + jnp.concatenate(o_inter[h], axis=0)
            go = go_ref[rows, cols[h]].astype(F32)
            o_ref[rows, cols[h]] = (_rms(o, nw) * (go * _sigmoid(go))).astype(o_ref.dtype)
        return carry

    lax.fori_loop(0, seq // SUPER, step, 0, unroll=2)


def _gla_mixer(proj, gates, w_gate, b_gate, norm_w):
    b, t, _ = proj.shape
    twice = lambda m: jnp.tile(m.reshape(m.shape[0], H_GLA, 1, GLA_DK), (1, 1, 2, 1)).reshape(m.shape[0], GLA_W)
    wg = jnp.zeros((HEAD, GLA_W), F32).at[LANE_GLA_LR:LANE_GLA_LR + GLA_RANK].set(twice(w_gate))
    bg = twice(b_gate.reshape(1, -1))
    act = lambda k: pl.BlockSpec((None, t, GLA_W), lambda bi: (bi, 0, OFF_GLA // GLA_W + k))
    const = lambda shape: pl.BlockSpec(shape, lambda bi: (0, 0))
    return pl.pallas_call(
        functools.partial(_gla_kernel, seq=t),
        grid=(b,),
        in_specs=[act(0), act(1), act(2), pl.BlockSpec((None, t, HEAD), lambda bi: (bi, 0, 0)),
                  const((HEAD, GLA_W)), const((1, GLA_W)), const((1, HEAD))],
        out_specs=pl.BlockSpec((None, t, GLA_W), lambda bi: (bi, 0, 0)),
        out_shape=jax.ShapeDtypeStruct((b, t, GLA_W), BF16),
        scratch_shapes=[pltpu.VMEM((H_GLA, HEAD, HEAD), F32)],
        compiler_params=_params(("parallel",)),
        name="gla_mixer",
    )(proj, proj, proj, gates, wg, bg, norm_w.reshape(1, HEAD))


FOX_SPAN = 512
FOX_ROWS = 256


def _fox_kernel(q_ref, k_ref, v_ref, g_ref, fb_ref, o_ref, c_ref, ct_ref, vaug_ref, *, seq):
    heads = range(H_FOX)
    cols = [slice(h * HEAD, (h + 1) * HEAD) for h in heads]
    ones_col = (lax.broadcasted_iota(jnp.int32, (seq, HEAD), 1) == 0).astype(BF16)
    for h in heads:
        vaug_ref[:, 2 * h * HEAD:(2 * h + 1) * HEAD] = v_ref[:, cols[h]]
        vaug_ref[:, (2 * h + 1) * HEAD:(2 * h + 2) * HEAD] = ones_col
    lanes = [slice(LANE_FOX_F + h, LANE_FOX_F + h + 1) for h in heads]
    row = lax.broadcasted_iota(jnp.int32, (SUPER, SUPER), 0)
    col = lax.broadcasted_iota(jnp.int32, (SUPER, SUPER), 1)
    tril = (row >= col).astype(BF16)
    fb = fb_ref[...]
    run = jnp.zeros((1, HEAD), F32)
    for i in range(seq // SUPER):
        rs = slice(i * SUPER, (i + 1) * SUPER)
        cb = _dot_exact_lhs(tril, _log_sigmoid(g_ref[rs, :] + fb)) + run
        c_ref[rs, :] = cb
        ct_ref[:, rs] = cb.T
        run = cb[SUPER - 1:SUPER, :]

    qrow = lax.broadcasted_iota(jnp.int32, (FOX_ROWS, FOX_SPAN), 0)
    kcol = lax.broadcasted_iota(jnp.int32, (FOX_ROWS, FOX_SPAN), 1)
    scale = HEAD ** -0.5

    for span in range(seq // FOX_SPAN):
        hi = (span + 1) * FOX_SPAN
        lo = hi - FOX_SPAN

        def q_step(sb, carry, hi=hi, lo=lo):
            r0 = pl.multiple_of(lo + sb * FOX_ROWS, FOX_ROWS)
            cq_all = c_ref[pl.ds(r0, FOX_ROWS), :] * LOG2E
            causal = qrow + sb * FOX_ROWS >= kcol
            s = [_dot_nt(q_ref[pl.ds(r0, FOX_ROWS), cols[h]], k_ref[0:hi, cols[h]]) for h in heads]
            s = [s[h] * (scale * LOG2E) + (cq_all[:, lanes[h]] - ct_ref[lanes[h], 0:hi] * LOG2E) for h in heads]
            tail = [jnp.where(causal, sh[:, lo:], -jnp.inf) for sh in s]
            s = [jnp.concatenate([s[h][:, :lo], tail[h]], axis=1) for h in heads] if lo else tail
            p = [jnp.exp2(sh - jnp.max(sh, axis=1, keepdims=True)).astype(BF16) for sh in s]
            pv = [_dot(p[h], vaug_ref[0:hi, 2 * h * HEAD:(2 * h + 2) * HEAD]) for h in heads]
            for h in heads:
                o_ref[pl.ds(r0, FOX_ROWS), cols[h]] = (pv[h][:, :HEAD] / pv[h][:, HEAD:HEAD + 1]).astype(o_ref.dtype)
            return carry

        lax.fori_loop(0, FOX_SPAN // FOX_ROWS, q_step, 0, unroll=True)


def _fox_mixer(proj, gates, f_bias):
    b, t, _ = proj.shape
    fb = jnp.zeros((1, HEAD), F32).at[0, LANE_FOX_F:LANE_FOX_F + H_FOX].set(f_bias)
    act = lambda k: pl.BlockSpec((None, t, FOX_W), lambda bi: (bi, 0, OFF_FOX // FOX_W + k))
    return pl.pallas_call(
        functools.partial(_fox_kernel, seq=t),
        grid=(b,),
        in_specs=[act(0), act(1), act(2), pl.BlockSpec((None, t, HEAD), lambda bi: (bi, 0, 0)),
                  pl.BlockSpec((1, HEAD), lambda bi: (0, 0))],
        out_specs=pl.BlockSpec((None, t, FOX_W), lambda bi: (bi, 0, 0)),
        out_shape=jax.ShapeDtypeStruct((b, t, FOX_W), BF16),
        scratch_shapes=[pltpu.VMEM((t, HEAD), F32), pltpu.VMEM((HEAD, t), F32), pltpu.VMEM((t, 2 * FOX_W), BF16)],
        compiler_params=_params(("parallel",)),
        name="fox_mixer",
    )(proj, proj, proj, gates, fb)


OUT_ROWS = 256


def _outproj_kernel(og_ref, ol_ref, of_ref, x_ref, w_ref, npost_ref, npre_ref, x1_ref, h_ref, cat_ref):
    cat_ref[:, :GDN_W] = og_ref[...]
    cat_ref[:, GDN_W:GDN_W + GLA_W] = ol_ref[...]
    cat_ref[:, GDN_W + GLA_W:] = of_ref[...]
    w = w_ref[...]
    for r in range(cat_ref.shape[0] // OUT_ROWS):
        rows = slice(r * OUT_ROWS, (r + 1) * OUT_ROWS)
        x1 = x_ref[rows, :] + _rms(_dot(cat_ref[rows, :], w), npost_ref[...])
        x1_ref[rows, :] = x1
        h_ref[rows, :] = _rms(x1, npre_ref[...]).astype(h_ref.dtype)


def _outproj(o_gdn, o_gla, o_fox, x, w_out, layer, n_post, n_pre_next, tm=512):
    n, d = x.shape
    rowblk = lambda wd: pl.BlockSpec((tm, wd), lambda i: (i, 0))
    const = lambda shape: pl.BlockSpec(shape, lambda i: (0, 0))
    return pl.pallas_call(
        _outproj_kernel,
        grid=(n // tm,),
        in_specs=[rowblk(GDN_W), rowblk(GLA_W), rowblk(FOX_W), rowblk(d),
                  pl.BlockSpec((None, d, d), lambda i: (layer, 0, 0)), const((1, d)), const((1, d))],
        out_specs=[rowblk(d), rowblk(d)],
        out_shape=[jax.ShapeDtypeStruct((n, d), F32), jax.ShapeDtypeStruct((n, d), BF16)],
        scratch_shapes=[pltpu.VMEM((tm, d), BF16)],
        compiler_params=_params(("parallel",)),
        name="out_proj",
    )(o_gdn, o_gla, o_fox, x, w_out, n_post.reshape(1, d), n_pre_next.reshape(1, d))


def _gelu_tanh(x):
    k = 2.0 * math.sqrt(2.0 / math.pi)
    return x / (1.0 + jnp.exp(x * ((-k * 0.044715) * (x * x) - k)))


FFN_ROWS = 256


def _ffn_up_kernel(h_ref, wg_ref, wv_ref, cg_ref, cv_ref, bg_ref, bv_ref, o_ref):
    seq, tf = o_ref.shape
    wg = wg_ref[...].astype(BF16)
    wv = wv_ref[...].astype(BF16)

    def conv(u, tail, c_ref, b_ref):
        c = c_ref[...]
        ue = jnp.concatenate([tail, u], axis=0)
        return (u * c[2:3, :] + pltpu.roll(ue, 1, 0)[8:, :] * c[1:2, :] + pltpu.roll(ue, 2, 0)[8:, :] * c[0:1, :]
                + b_ref[...])

    tail_g = jnp.zeros((8, tf), F32)
    tail_v = jnp.zeros((8, tf), F32)
    for r in range(seq // FFN_ROWS):
        rows = slice(r * FFN_ROWS, (r + 1) * FFN_ROWS)
        h = h_ref[rows, :]
        ug = _dot(h, wg)
        uv = _dot(h, wv)
        o_ref[rows, :] = (_gelu_tanh(conv(ug, tail_g, cg_ref, bg_ref)) * conv(uv, tail_v, cv_ref, bv_ref)).astype(o_ref.dtype)
        tail_g, tail_v = ug[FFN_ROWS - 8:, :], uv[FFN_ROWS - 8:, :]


def _ffn_up(h, w_up, conv_w, conv_b, layer, seq, tf=1024):
    n, d = h.shape
    nf = D_FF // tf
    conv_b = conv_b.reshape(conv_b.shape[0], 1, -1)
    wspec = lambda off: pl.BlockSpec((None, d, tf), lambda i, j: (layer, 0, off + j))
    cspec = lambda rows, off: pl.BlockSpec((None, rows, tf), lambda i, j: (layer, 0, off + j))
    return pl.pallas_call(
        _ffn_up_kernel,
        grid=(n // seq, nf),
        in_specs=[pl.BlockSpec((seq, d), lambda i, j: (i, 0), pipeline_mode=pl.Buffered(1)), wspec(0), wspec(nf),
                  cspec(3, 0), cspec(3, nf), cspec(1, 0), cspec(1, nf)],
        out_specs=pl.BlockSpec((seq, tf), lambda i, j: (i, j)),
        out_shape=jax.ShapeDtypeStruct((n, D_FF), BF16),
        compiler_params=_params(("parallel", "parallel"), vmem=FFN_UP_VMEM),
        name="ffn_up",
    )(h, w_up, w_up, conv_w, conv_w, conv_b, conv_b)


def _ffn_down_kernel(a_ref, w_ref, x_ref, npost_ref, *rest):
    x2 = x_ref[...] + _rms(_dot(a_ref[...], w_ref[...]), npost_ref[...])
    if len(rest) == 1:
        rest[0][...] = x2
    else:
        nnext_ref, x2_ref, h_ref = rest
        x2_ref[...] = x2
        h_ref[...] = _rms(x2, nnext_ref[...]).astype(h_ref.dtype)


def _ffn_down(a, w_down, layer, x, n_post, n_next=None, tm=256):
    n, d = x.shape
    kdim = a.shape[1]
    const = pl.BlockSpec((1, d), lambda i: (0, 0))
    rowblk = pl.BlockSpec((tm, d), lambda i: (i, 0))
    in_specs = [pl.BlockSpec((tm, kdim), lambda i: (i, 0)),
                pl.BlockSpec((None, kdim, d), lambda i: (layer, 0, 0), pipeline_mode=pl.Buffered(1)), rowblk, const]
    args = [a, w_down, x, n_post.reshape(1, d)]
    x_out = jax.ShapeDtypeStruct((n, d), F32)
    if n_next is None:
        out_specs, out_shape = rowblk, x_out
    else:
        in_specs.append(const)
        args.append(n_next.reshape(1, d))
        out_specs, out_shape = [rowblk, rowblk], [x_out, jax.ShapeDtypeStruct((n, d), BF16)]
    return pl.pallas_call(
        _ffn_down_kernel,
        grid=(n // tm,),
        in_specs=in_specs,
        out_specs=out_specs,
        out_shape=out_shape,
        compiler_params=_params(("parallel",)),
        name="ffn_down",
    )(*args)


def _pack_in_proj(w_in):
    wt = jnp.swapaxes(w_in, 1, 2)
    depth, _, d = wt.shape
    o = 0
    segs = {}
    for name, width in (("gdn", 4 * GDN_W), ("gdn_b", H_GDN), ("gdn_a", H_GDN), ("gla_q", H_GLA * GLA_DK),
                        ("gla_k", H_GLA * GLA_DK), ("gla_vg", 2 * GLA_W), ("gla_lr", GLA_RANK),
                        ("fox", 3 * FOX_W), ("fox_f", H_FOX)):
        segs[name] = wt[:, o:o + width]
        o += width
    per_head = lambda m: m.reshape(depth, H_GLA, GLA_DK, d)
    gla_qk = jnp.concatenate([per_head(segs["gla_q"]), per_head(segs["gla_k"])], axis=2).reshape(depth, GLA_W, d)
    zeros = lambda n: jnp.zeros((depth, n, d), wt.dtype)
    qkv = segs["gdn"][:, :QKV_W]
    rest = jnp.concatenate([gla_qk, segs["gla_vg"], segs["fox"], segs["gdn"][:, QKV_W:]], axis=1)
    small = jnp.concatenate([segs["gdn_b"], segs["gdn_a"], segs["gla_lr"], segs["fox_f"],
                             zeros(HEAD - 2 * H_GDN - GLA_RANK - H_FOX)], axis=1)
    return qkv.astype(BF16), rest.astype(BF16), small.astype(BF16)


def kernel(x, w_in, conv_gdn, gdn_a_log, gdn_dt_bias, gdn_norm, gla_w_gate, gla_b_gate, gla_norm, fox_f_bias, w_out, norm_pre_mix, norm_post_mix, norm_pre_ffn, norm_post_ffn, w_up, conv_ffn, conv_ffn_bias, w_down):
    b, t, d = x.shape
    n = b * t
    depth = w_in.shape[0]
    xf = x.reshape(n, d)
    w_qkv, w_rest, w_small = _pack_in_proj(w_in)
    w_out_b, w_down_b = w_out.astype(BF16), w_down.astype(BF16)
    h = _rmsnorm(xf, norm_pre_mix[0])
    for i in range(depth):
        qkv = _in_proj_conv(h, w_qkv, conv_gdn, i, seq=t).reshape(b, t, QKV_W)
        rest = _matmul_nt(h, w_rest, i, BF16, tm=1024, tn=REST_W // 3, name="in_proj").reshape(b, t, REST_W)
        gates = _matmul_nt(h, w_small, i, F32, tm=1024, tn=HEAD, name="in_proj_gates").reshape(b, t, HEAD)
        o_gdn = _gdn_mixer(qkv, rest, gates, gdn_a_log[i], gdn_dt_bias[i], gdn_norm[i])
        o_gla = _gla_mixer(rest, gates, gla_w_gate[i], gla_b_gate[i], gla_norm[i])
        o_fox = _fox_mixer(rest, gates, fox_f_bias[i])
        xf, h = _outproj(o_gdn.reshape(n, GDN_W), o_gla.reshape(n, GLA_W), o_fox.reshape(n, FOX_W), xf,
                         w_out_b, i, norm_post_mix[i], norm_pre_ffn[i])
        a = _ffn_up(h, w_up, conv_ffn, conv_ffn_bias, i, seq=t)
        if i + 1 < depth:
            xf, h = _ffn_down(a, w_down_b, i, xf, norm_post_ffn[i], norm_pre_mix[i + 1])
        else:
            xf = _ffn_down(a, w_down_b, i, xf, norm_post_ffn[i])
    return xf.reshape(b, t, d)
```

```python
import functools
import math

import jax
import jax.numpy as jnp
from jax import lax
from jax.experimental import pallas as pl
from jax.experimental.pallas import tpu as pltpu

F32 = jnp.float32
BF16 = jnp.bfloat16

D_MODEL = 2048
HEAD = 128
H_GDN, H_GLA, H_FOX = 6, 5, 5
GLA_DK = 64
GDN_CONV = 4
GLA_RANK = 16
GLA_NORMALIZER = 16.0
CHUNK = 64
SUPER = 256
D_FF = 4 * D_MODEL
EPS = 1e-6
LOG2E = math.log2(math.e)

GDN_W = H_GDN * HEAD
GLA_W = H_GLA * HEAD
FOX_W = H_FOX * HEAD
QKV_W = 3 * GDN_W
OFF_GLA = 0
OFF_FOX = 1920
OFF_GDN_Z = 3840
REST_W = 4608
LANE_GDN_B, LANE_GDN_A, LANE_GLA_LR, LANE_FOX_F = 0, 6, 12, 28

V7X_VMEM_LIMIT = 56 * 1024 * 1024


def _params(sem, vmem=V7X_VMEM_LIMIT, flags=None):
    return pltpu.CompilerParams(dimension_semantics=sem, vmem_limit_bytes=vmem, flags=flags)


def _dot(a, b):
    return jnp.dot(a, b, preferred_element_type=F32)


def _dot_nt(a, b):
    return lax.dot_general(a, b, (((1,), (1,)), ((), ())), preferred_element_type=F32)


def _dot_tn(a, b):
    return lax.dot_general(a, b, (((0,), (0,)), ((), ())), preferred_element_type=F32)


def _split3(x):
    x1 = x.astype(BF16)
    r = x - x1.astype(F32)
    x2 = r.astype(BF16)
    x3 = (r - x2.astype(F32)).astype(BF16)
    return x1, x2, x3


def _dot_exact_lhs(m, x):
    x1, x2, x3 = _split3(x)
    return _dot(m, x1) + _dot(m, x2) + _dot(m, x3)


def _sigmoid(x):
    return 1.0 / (1.0 + jnp.exp(-x))


def _softplus(x):
    return jnp.maximum(x, 0.0) + jnp.log1p(jnp.exp(-jnp.abs(x)))


def _log_sigmoid(x):
    return jnp.minimum(x, 0.0) - jnp.log1p(jnp.exp(-jnp.abs(x)))


def _rms(x, w):
    return x * lax.rsqrt(jnp.mean(x * x, axis=-1, keepdims=True) + EPS) * w


def _chunk_masks(n):
    row = lax.broadcasted_iota(jnp.int32, (n, n), 0)
    col = lax.broadcasted_iota(jnp.int32, (n, n), 1)
    same = (row // CHUNK) == (col // CHUNK)
    return same & (row >= col), same & (row > col)


def _rmsnorm_kernel(x_ref, w_ref, o_ref):
    o_ref[...] = _rms(x_ref[...], w_ref[...]).astype(o_ref.dtype)


def _rmsnorm(x, w, tm=512):
    n, d = x.shape
    return pl.pallas_call(
        _rmsnorm_kernel,
        grid=(n // tm,),
        in_specs=[pl.BlockSpec((tm, d), lambda i: (i, 0)), pl.BlockSpec((1, d), lambda i: (0, 0))],
        out_specs=pl.BlockSpec((tm, d), lambda i: (i, 0)),
        out_shape=jax.ShapeDtypeStruct((n, d), BF16),
        compiler_params=_params(("parallel",)),
        name="rmsnorm",
    )(x, w.reshape(1, d))


IN_ROWS = 256


def _in_proj_conv_kernel(a_ref, bt_ref, cw_ref, o_ref, tail_ref, *, tiles_per_seq):
    w = bt_ref[...]
    cw = cw_ref[...]

    @pl.when(pl.program_id(0) % tiles_per_seq == 0)
    def _():
        tail_ref[...] = jnp.zeros_like(tail_ref)

    tail = tail_ref[...]
    for r in range(a_ref.shape[0] // IN_ROWS):
        rows = slice(r * IN_ROWS, (r + 1) * IN_ROWS)
        u = _dot_nt(a_ref[rows, :], w)
        ue = jnp.concatenate([tail, u], axis=0)
        y = u * cw[GDN_CONV - 1:GDN_CONV, :]
        for t in range(1, GDN_CONV):
            y = y + pltpu.roll(ue, t, 0)[8:, :] * cw[GDN_CONV - 1 - t:GDN_CONV - t, :]
        o_ref[rows, :] = (y * _sigmoid(y)).astype(o_ref.dtype)
        tail = u[IN_ROWS - 8:, :]
    tail_ref[...] = tail


def _in_proj_conv(a, bt, conv_w, layer, seq):
    n, k = a.shape
    nc = bt.shape[1]
    tm = min(1024, seq)
    return pl.pallas_call(
        functools.partial(_in_proj_conv_kernel, tiles_per_seq=seq // tm),
        grid=(n // tm,),
        in_specs=[pl.BlockSpec((tm, k), lambda i: (i, 0)),
                  pl.BlockSpec((None, nc, k), lambda i: (layer, 0, 0)),
                  pl.BlockSpec((None, GDN_CONV, nc), lambda i: (layer, 0, 0))],
        out_specs=pl.BlockSpec((tm, nc), lambda i: (i, 0)),
        out_shape=jax.ShapeDtypeStruct((n, nc), BF16),
        scratch_shapes=[pltpu.VMEM((8, nc), F32)],
        compiler_params=_params(("arbitrary",)),
        name="in_proj_qkv",
    )(a, bt, conv_w)


def _mm_nt_kernel(a_ref, bt_ref, o_ref):
    o_ref[...] = _dot_nt(a_ref[...], bt_ref[...]).astype(o_ref.dtype)


def _matmul_nt(a, bt, layer, out_dtype, tm, tn, name):
    n, k = a.shape
    nc = bt.shape[1]
    return pl.pallas_call(
        _mm_nt_kernel,
        grid=(n // tm, nc // tn),
        in_specs=[pl.BlockSpec((tm, k), lambda i, j: (i, 0)),
                  pl.BlockSpec((None, tn, k), lambda i, j: (layer, j, 0))],
        out_specs=pl.BlockSpec((tm, tn), lambda i, j: (i, j)),
        out_shape=jax.ShapeDtypeStruct((n, nc), out_dtype),
        compiler_params=_params(("parallel", "parallel")),
        name=name,
    )(a, bt)


def _gdn_kernel(q_ref, k_ref, v_ref, z_ref, g_ref, alog_ref, dtb_ref, nw_ref, o_ref, state_ref, *, seq):
    heads = range(H_GDN)
    cols = [slice(h * HEAD, (h + 1) * HEAD) for h in heads]
    incl, strict = _chunk_masks(SUPER)
    tril_bd = incl.astype(BF16)
    neg_a = -jnp.exp(alog_ref[...])
    dtb = dtb_ref[...]
    nw = nw_ref[...]
    state_ref[...] = jnp.zeros_like(state_ref)

    def l2n(x):
        return x * lax.rsqrt(jnp.sum(x * x, axis=-1, keepdims=True) + EPS)

    def step(s, carry):
        r0 = pl.multiple_of(s * SUPER, SUPER)
        rows = pl.ds(r0, SUPER)
        gates = g_ref[rows, :]
        beta_all = _sigmoid(gates)
        g_all = neg_a * _softplus(gates + dtb)
        gc_all = _dot_exact_lhs(tril_bd, g_all)
        qn = [l2n(q_ref[rows, c].astype(F32)) * (HEAD ** -0.5) for c in cols]
        kn = [l2n(k_ref[rows, c].astype(F32)) for c in cols]
        vv = [v_ref[rows, c].astype(F32) for c in cols]

        beta = [beta_all[:, LANE_GDN_B + h:LANE_GDN_B + h + 1] for h in heads]
        gcb = [jnp.broadcast_to(gc_all[:, LANE_GDN_A + h:LANE_GDN_A + h + 1], (SUPER, SUPER)) for h in heads]
        gcl = [g[:, :HEAD] for g in gcb]
        decay = [jnp.where(incl, jnp.exp(g - g.T), 0.0) for g in gcb]
        kb = [kn[h] * beta[h] for h in heads]
        knb = [k.astype(BF16) for k in kn]
        sk = [_dot_nt(kb[h].astype(BF16), knb[h]) for h in heads]
        qk = [_dot_nt(qn[h].astype(BF16), knb[h]) for h in heads]
        a_mat = [jnp.where(strict, sk[h] * decay[h], 0.0) for h in heads]
        qk = [qk[h] * decay[h] for h in heads]
        rhs = [jnp.concatenate([vv[h] * beta[h], kb[h] * jnp.exp(gcl[h])], axis=1) for h in heads]
        p = [a.astype(BF16) for a in a_mat]
        x = [rhs[h] - _dot(p[h], rhs[h].astype(BF16)) for h in heads]
        for _ in range(5):
            p = [_dot(pi, pi).astype(BF16) for pi in p]
            x = [x[h] + _dot(p[h], x[h].astype(BF16)) for h in heads]
        u = [xi[:, :HEAD] for xi in x]
        w = [xi[:, HEAD:].astype(BF16) for xi in x]
        qg = [(qn[h] * jnp.exp(gcl[h])).astype(BF16) for h in heads]

        st = [state_ref[h] for h in heads]
        v_new = [[] for _ in heads]
        o_inter = [[] for _ in heads]
        for c in range(SUPER // CHUNK):
            rs = slice(c * CHUNK, (c + 1) * CHUNK)
            last = slice((c + 1) * CHUNK - 1, (c + 1) * CHUNK)
            stb = [t.astype(BF16) for t in st]
            vn = [u[h][rs] - _dot(w[h][rs], stb[h]) for h in heads]
            oi = [_dot(qg[h][rs], stb[h]) for h in heads]
            ke = [(kn[h][rs] * jnp.exp(gcl[h][last] - gcl[h][rs])).astype(BF16) for h in heads]
            st = [st[h] * jnp.exp(gcl[h][last]) + _dot_tn(ke[h], vn[h].astype(BF16)) for h in heads]
            for h in heads:
                v_new[h].append(vn[h])
                o_inter[h].append(oi[h])
        for h in heads:
            state_ref[h] = st[h]
        o = [jnp.concatenate(o_inter[h], axis=0)
             + _dot(qk[h].astype(BF16), jnp.concatenate(v_new[h], axis=0).astype(BF16)) for h in heads]
        for h in heads:
            z = z_ref[rows, cols[h]].astype(F32)
            o_ref[rows, cols[h]] = (_rms(o[h], nw) * (z * _sigmoid(z))).astype(o_ref.dtype)
        return carry

    lax.fori_loop(0, seq // SUPER, step, 0, unroll=4)


def _gdn_mixer(qkv, rest, gates, a_log, dt_bias, norm_w):
    b, t, _ = qkv.shape
    row = lambda v, off: jnp.zeros((1, HEAD), F32).at[0, off:off + v.shape[0]].set(v)
    act = lambda k: pl.BlockSpec((None, t, GDN_W), lambda bi: (bi, 0, k))
    vec = pl.BlockSpec((1, HEAD), lambda bi: (0, 0))
    return pl.pallas_call(
        functools.partial(_gdn_kernel, seq=t),
        grid=(b,),
        in_specs=[act(0), act(1), act(2), act(OFF_GDN_Z // GDN_W), pl.BlockSpec((None, t, HEAD), lambda bi: (bi, 0, 0)),
                  vec, vec, vec],
        out_specs=pl.BlockSpec((None, t, GDN_W), lambda bi: (bi, 0, 0)),
        out_shape=jax.ShapeDtypeStruct((b, t, GDN_W), BF16),
        scratch_shapes=[pltpu.VMEM((H_GDN, HEAD, HEAD), F32)],
        compiler_params=_params(("parallel",)),
        name="gdn_mixer",
    )(qkv, qkv, qkv, rest, gates, row(a_log, LANE_GDN_A), row(dt_bias, LANE_GDN_A), norm_w.reshape(1, HEAD))


def _gla_kernel(qk_ref, v_ref, go_ref, g_ref, wg_ref, bg_ref, nw_ref, o_ref, state_ref, *, seq):
    heads = range(H_GLA)
    cols = [slice(h * HEAD, (h + 1) * HEAD) for h in heads]
    incl, _ = _chunk_masks(SUPER)
    tril_bd = incl.astype(BF16)
    is_q = lax.broadcasted_iota(jnp.int32, (1, HEAD), 1) < GLA_DK
    sign = jnp.where(is_q, 1.0, -1.0)
    q_scale = jnp.where(is_q, GLA_DK ** -0.5, 1.0)
    nw = nw_ref[...]
    wg = wg_ref[...]
    wg_hi = wg.astype(BF16)
    wg_lo = (wg - wg_hi.astype(F32)).astype(BF16)
    bg = bg_ref[...]
    state_ref[...] = jnp.zeros_like(state_ref)

    def k_half(x):
        return jnp.where(is_q, pltpu.roll(x, GLA_DK, 1), 0.0).astype(BF16)

    def step(s, carry):
        r0 = pl.multiple_of(s * SUPER, SUPER)
        rows = pl.ds(r0, SUPER)
        lr = g_ref[rows, :]
        lr_hi = lr.astype(BF16)
        lr_lo = (lr - lr_hi.astype(F32)).astype(BF16)
        logits = _dot(lr_hi, wg_hi) + _dot(lr_hi, wg_lo) + _dot(lr_lo, wg_hi) + bg
        gk_all = _log_sigmoid(logits) / GLA_NORMALIZER
        bc = [_dot_exact_lhs(tril_bd, gk_all[:, c]) for c in cols]
        x = [qk_ref[rows, c].astype(F32) for c in cols]
        v = [v_ref[rows, c] for c in cols]
        xe = [x[h] * (jnp.exp(bc[h] * sign) * q_scale) for h in heads]
        q_dec = [jnp.where(is_q, xe[h], 0.0).astype(BF16) for h in heads]
        k_inv = [k_half(xe[h]) for h in heads]
        attn = [jnp.where(incl, _dot_nt(q_dec[h], k_inv[h]), 0.0).astype(BF16) for h in heads]
        o_intra = [_dot(attn[h], v[h]) for h in heads]
        st = [state_ref[h] for h in heads]
        o_inter = [[] for _ in heads]
        for c in range(SUPER // CHUNK):
            rs = slice(c * CHUNK, (c + 1) * CHUNK)
            last = slice((c + 1) * CHUNK - 1, (c + 1) * CHUNK)
            oi = [_dot_nt(q_dec[h][rs], st[h].astype(BF16)) for h in heads]
            k_end = [k_half(x[h][rs] * jnp.exp(bc[h][last] - bc[h][rs])) for h in heads]
            st = [st[h] * jnp.exp(bc[h][last]) + _dot_tn(v[h][rs], k_end[h]) for h in heads]
            for h in heads:
                o_inter[h].append(oi[h])
        for h in heads:
            state_ref[h] = st[h]
            o = o_intra[h] + jnp.concatenate(o_inter[h], axis=0)
            go = go_ref[rows, cols[h]].astype(F32)
            o_ref[rows, cols[h]] = (_rms(o, nw) * (go * _sigmoid(go))).astype(o_ref.dtype)
        return carry

    lax.fori_loop(0, seq // SUPER, step, 0, unroll=4)


def _gla_mixer(proj, gates, w_gate, b_gate, norm_w):
    b, t, _ = proj.shape
    twice = lambda m: jnp.tile(m.reshape(m.shape[0], H_GLA, 1, GLA_DK), (1, 1, 2, 1)).reshape(m.shape[0], GLA_W)
    wg = jnp.zeros((HEAD, GLA_W), F32).at[LANE_GLA_LR:LANE_GLA_LR + GLA_RANK].set(twice(w_gate))
    bg = twice(b_gate.reshape(1, -1))
    act = lambda k: pl.BlockSpec((None, t, GLA_W), lambda bi: (bi, 0, OFF_GLA // GLA_W + k))
    const = lambda shape: pl.BlockSpec(shape, lambda bi: (0, 0))
    return pl.pallas_call(
        functools.partial(_gla_kernel, seq=t),
        grid=(b,),
        in_specs=[act(0), act(1), act(2), pl.BlockSpec((None, t, HEAD), lambda bi: (bi, 0, 0)),
                  const((HEAD, GLA_W)), const((1, GLA_W)), const((1, HEAD))],
        out_specs=pl.BlockSpec((None, t, GLA_W), lambda bi: (bi, 0, 0)),
        out_shape=jax.ShapeDtypeStruct((b, t, GLA_W), BF16),
        scratch_shapes=[pltpu.VMEM((H_GLA, HEAD, HEAD), F32)],
        compiler_params=_params(("parallel",)),
        name="gla_mixer",
    )(proj, proj, proj, gates, wg, bg, norm_w.reshape(1, HEAD))


FOX_SPAN = 512
FOX_ROWS = 256


def _fox_kernel(q_ref, k_ref, v_ref, g_ref, fb_ref, o_ref, c_ref, ct_ref, vaug_ref, *, seq):
    heads = range(H_FOX)
    cols = [slice(h * HEAD, (h + 1) * HEAD) for h in heads]
    ones_col = (lax.broadcasted_iota(jnp.int32, (seq, HEAD), 1) == 0).astype(BF16)
    for h in heads:
        vaug_ref[:, 2 * h * HEAD:(2 * h + 1) * HEAD] = v_ref[:, cols[h]]
        vaug_ref[:, (2 * h + 1) * HEAD:(2 * h + 2) * HEAD] = ones_col
    lanes = [slice(LANE_FOX_F + h, LANE_FOX_F + h + 1) for h in heads]
    row = lax.broadcasted_iota(jnp.int32, (SUPER, SUPER), 0)
    col = lax.broadcasted_iota(jnp.int32, (SUPER, SUPER), 1)
    tril = (row >= col).astype(BF16)
    fb = fb_ref[...]
    run = jnp.zeros((1, HEAD), F32)
    for i in range(seq // SUPER):
        rs = slice(i * SUPER, (i + 1) * SUPER)
        cb = _dot_exact_lhs(tril, _log_sigmoid(g_ref[rs, :] + fb)) + run
        c_ref[rs, :] = cb
        ct_ref[:, rs] = cb.T
        run = cb[SUPER - 1:SUPER, :]

    qrow = lax.broadcasted_iota(jnp.int32, (FOX_ROWS, FOX_SPAN), 0)
    kcol = lax.broadcasted_iota(jnp.int32, (FOX_ROWS, FOX_SPAN), 1)
    scale = HEAD ** -0.5

    for span in range(seq // FOX_SPAN):
        hi = (span + 1) * FOX_SPAN
        lo = hi - FOX_SPAN

        def q_step(sb, carry, hi=hi, lo=lo):
            r0 = pl.multiple_of(lo + sb * FOX_ROWS, FOX_ROWS)
            cq_all = c_ref[pl.ds(r0, FOX_ROWS), :] * LOG2E
            causal = qrow + sb * FOX_ROWS >= kcol
            s = [_dot_nt(q_ref[pl.ds(r0, FOX_ROWS), cols[h]], k_ref[0:hi, cols[h]]) for h in heads]
            s = [s[h] * (scale * LOG2E) + (cq_all[:, lanes[h]] - ct_ref[lanes[h], 0:hi] * LOG2E) for h in heads]
            tail = [jnp.where(causal, sh[:, lo:], -jnp.inf) for sh in s]
            s = [jnp.concatenate([s[h][:, :lo], tail[h]], axis=1) for h in heads] if lo else tail
            p = [jnp.exp2(sh - jnp.max(sh, axis=1, keepdims=True)).astype(BF16) for sh in s]
            pv = [_dot(p[h], vaug_ref[0:hi, 2 * h * HEAD:(2 * h + 2) * HEAD]) for h in heads]
            for h in heads:
                o_ref[pl.ds(r0, FOX_ROWS), cols[h]] = (pv[h][:, :HEAD] / pv[h][:, HEAD:HEAD + 1]).astype(o_ref.dtype)
            return carry

        lax.fori_loop(0, FOX_SPAN // FOX_ROWS, q_step, 0, unroll=True)


def _fox_mixer(proj, gates, f_bias):
    b, t, _ = proj.shape
    fb = jnp.zeros((1, HEAD), F32).at[0, LANE_FOX_F:LANE_FOX_F + H_FOX].set(f_bias)
    act = lambda k: pl.BlockSpec((None, t, FOX_W), lambda bi: (bi, 0, OFF_FOX // FOX_W + k))
    return pl.pallas_call(
        functools.partial(_fox_kernel, seq=t),
        grid=(b,),
        in_specs=[act(0), act(1), act(2), pl.BlockSpec((None, t, HEAD), lambda bi: (bi, 0, 0)),
                  pl.BlockSpec((1, HEAD), lambda bi: (0, 0))],
        out_specs=pl.BlockSpec((None, t, FOX_W), lambda bi: (bi, 0, 0)),
        out_shape=jax.ShapeDtypeStruct((b, t, FOX_W), BF16),
        scratch_shapes=[pltpu.VMEM((t, HEAD), F32), pltpu.VMEM((HEAD, t), F32), pltpu.VMEM((t, 2 * FOX_W), BF16)],
        compiler_params=_params(("parallel",)),
        name="fox_mixer",
    )(proj, proj, proj, gates, fb)


OUT_ROWS = 128


def _outproj_kernel(og_ref, ol_ref, of_ref, x_ref, w_ref, npost_ref, npre_ref, x1_ref, h_ref, cat_ref):
    cat_ref[:, :GDN_W] = og_ref[...]
    cat_ref[:, GDN_W:GDN_W + GLA_W] = ol_ref[...]
    cat_ref[:, GDN_W + GLA_W:] = of_ref[...]
    w = w_ref[...]
    for r in range(cat_ref.shape[0] // OUT_ROWS):
        rows = slice(r * OUT_ROWS, (r + 1) * OUT_ROWS)
        x1 = x_ref[rows, :] + _rms(_dot(cat_ref[rows, :], w), npost_ref[...])
        x1_ref[rows, :] = x1
        h_ref[rows, :] = _rms(x1, npre_ref[...]).astype(h_ref.dtype)


def _outproj(o_gdn, o_gla, o_fox, x, w_out, layer, n_post, n_pre_next, tm=512):
    n, d = x.shape
    rowblk = lambda wd: pl.BlockSpec((tm, wd), lambda i: (i, 0))
    const = lambda shape: pl.BlockSpec(shape, lambda i: (0, 0))
    return pl.pallas_call(
        _outproj_kernel,
        grid=(n // tm,),
        in_specs=[rowblk(GDN_W), rowblk(GLA_W), rowblk(FOX_W), rowblk(d),
                  pl.BlockSpec((None, d, d), lambda i: (layer, 0, 0)), const((1, d)), const((1, d))],
        out_specs=[rowblk(d), rowblk(d)],
        out_shape=[jax.ShapeDtypeStruct((n, d), F32), jax.ShapeDtypeStruct((n, d), BF16)],
        scratch_shapes=[pltpu.VMEM((tm, d), BF16)],
        compiler_params=_params(("parallel",)),
        name="out_proj",
    )(o_gdn, o_gla, o_fox, x, w_out, n_post.reshape(1, d), n_pre_next.reshape(1, d))


def _gelu_tanh(x):
    k = 2.0 * math.sqrt(2.0 / math.pi)
    return x / (1.0 + jnp.exp(x * ((-k * 0.044715) * (x * x) - k)))


FFN_ROWS = 128


def _ffn_up_kernel(h_ref, wg_ref, wv_ref, cg_ref, cv_ref, bg_ref, bv_ref, o_ref):
    seq, tf = o_ref.shape
    wg = wg_ref[...]
    wv = wv_ref[...]

    def conv(u, tail, c_ref, b_ref):
        c = c_ref[...]
        ue = jnp.concatenate([tail, u], axis=0)
        return (u * c[2:3, :] + pltpu.roll(ue, 1, 0)[8:, :] * c[1:2, :] + pltpu.roll(ue, 2, 0)[8:, :] * c[0:1, :]
                + b_ref[...])

    tail_g = jnp.zeros((8, tf), F32)
    tail_v = jnp.zeros((8, tf), F32)
    for r in range(seq // FFN_ROWS):
        rows = slice(r * FFN_ROWS, (r + 1) * FFN_ROWS)
        h = h_ref[rows, :]
        ug = _dot(h, wg)
        uv = _dot(h, wv)
        o_ref[rows, :] = (_gelu_tanh(conv(ug, tail_g, cg_ref, bg_ref)) * conv(uv, tail_v, cv_ref, bv_ref)).astype(o_ref.dtype)
        tail_g, tail_v = ug[FFN_ROWS - 8:, :], uv[FFN_ROWS - 8:, :]


def _ffn_up(h, w_up, conv_w, conv_b, layer, seq, tf=1024):
    n, d = h.shape
    nf = D_FF // tf
    conv_b = conv_b.reshape(conv_b.shape[0], 1, -1)
    wspec = lambda off: pl.BlockSpec((None, d, tf), lambda i, j: (layer, 0, off + j))
    cspec = lambda rows, off: pl.BlockSpec((None, rows, tf), lambda i, j: (layer, 0, off + j))
    return pl.pallas_call(
        _ffn_up_kernel,
        grid=(n // seq, nf),
        in_specs=[pl.BlockSpec((seq, d), lambda i, j: (i, 0)), wspec(0), wspec(nf),
                  cspec(3, 0), cspec(3, nf), cspec(1, 0), cspec(1, nf)],
        out_specs=pl.BlockSpec((seq, tf), lambda i, j: (i, j)),
        out_shape=jax.ShapeDtypeStruct((n, D_FF), BF16),
        compiler_params=_params(("parallel", "parallel")),
        name="ffn_up",
    )(h, w_up, w_up, conv_w, conv_w, conv_b, conv_b)


def _ffn_down_kernel(a_ref, w_ref, x_ref, npost_ref, *rest):
    x2 = x_ref[...] + _rms(_dot(a_ref[...], w_ref[...]), npost_ref[...])
    if len(rest) == 1:
        rest[0][...] = x2
    else:
        nnext_ref, x2_ref, h_ref = rest
        x2_ref[...] = x2
        h_ref[...] = _rms(x2, nnext_ref[...]).astype(h_ref.dtype)


def _ffn_down(a, w_down, layer, x, n_post, n_next=None, tm=256):
    n, d = x.shape
    kdim = a.shape[1]
    const = pl.BlockSpec((1, d), lambda i: (0, 0))
    rowblk = pl.BlockSpec((tm, d), lambda i: (i, 0))
    in_specs = [pl.BlockSpec((tm, kdim), lambda i: (i, 0)),
                pl.BlockSpec((None, kdim, d), lambda i: (layer, 0, 0), pipeline_mode=pl.Buffered(1)), rowblk, const]
    args = [a, w_down, x, n_post.reshape(1, d)]
    x_out = jax.ShapeDtypeStruct((n, d), F32)
    if n_next is None:
        out_specs, out_shape = rowblk, x_out
    else:
        in_specs.append(const)
        args.append(n_next.reshape(1, d))
        out_specs, out_shape = [rowblk, rowblk], [x_out, jax.ShapeDtypeStruct((n, d), BF16)]
    return pl.pallas_call(
        _ffn_down_kernel,
        grid=(n // tm,),
        in_specs=in_specs,
        out_specs=out_specs,
        out_shape=out_shape,
        compiler_params=_params(("parallel",)),
        name="ffn_down",
    )(*args)


def _pack_in_proj(w_in):
    wt = jnp.swapaxes(w_in, 1, 2)
    depth, _, d = wt.shape
    o = 0
    segs = {}
    for name, width in (("gdn", 4 * GDN_W), ("gdn_b", H_GDN), ("gdn_a", H_GDN), ("gla_q", H_GLA * GLA_DK),
                        ("gla_k", H_GLA * GLA_DK), ("gla_vg", 2 * GLA_W), ("gla_lr", GLA_RANK),
                        ("fox", 3 * FOX_W), ("fox_f", H_FOX)):
        segs[name] = wt[:, o:o + width]
        o += width
    per_head = lambda m: m.reshape(depth, H_GLA, GLA_DK, d)
    gla_qk = jnp.concatenate([per_head(segs["gla_q"]), per_head(segs["gla_k"])], axis=2).reshape(depth, GLA_W, d)
    zeros = lambda n: jnp.zeros((depth, n, d), wt.dtype)
    qkv = segs["gdn"][:, :QKV_W]
    rest = jnp.concatenate([gla_qk, segs["gla_vg"], segs["fox"], segs["gdn"][:, QKV_W:]], axis=1)
    small = jnp.concatenate([segs["gdn_b"], segs["gdn_a"], segs["gla_lr"], segs["fox_f"],
                             zeros(HEAD - 2 * H_GDN - GLA_RANK - H_FOX)], axis=1)
    return qkv.astype(BF16), rest.astype(BF16), small.astype(BF16)


def kernel(x, w_in, conv_gdn, gdn_a_log, gdn_dt_bias, gdn_norm, gla_w_gate, gla_b_gate, gla_norm, fox_f_bias, w_out, norm_pre_mix, norm_post_mix, norm_pre_ffn, norm_post_ffn, w_up, conv_ffn, conv_ffn_bias, w_down):
    b, t, d = x.shape
    n = b * t
    depth = w_in.shape[0]
    xf = x.reshape(n, d)
    w_qkv, w_rest, w_small = _pack_in_proj(w_in)
    w_out_b, w_up_b, w_down_b = w_out.astype(BF16), w_up.astype(BF16), w_down.astype(BF16)
    h = _rmsnorm(xf, norm_pre_mix[0])
    for i in range(depth):
        qkv = _in_proj_conv(h, w_qkv, conv_gdn, i, seq=t).reshape(b, t, QKV_W)
        rest = _matmul_nt(h, w_rest, i, BF16, tm=1024, tn=REST_W // 3, name="in_proj").reshape(b, t, REST_W)
        gates = _matmul_nt(h, w_small, i, F32, tm=1024, tn=HEAD, name="in_proj_gates").reshape(b, t, HEAD)
        o_gdn = _gdn_mixer(qkv, rest, gates, gdn_a_log[i], gdn_dt_bias[i], gdn_norm[i])
        o_gla = _gla_mixer(rest, gates, gla_w_gate[i], gla_b_gate[i], gla_norm[i])
        o_fox = _fox_mixer(rest, gates, fox_f_bias[i])
        xf, h = _outproj(o_gdn.reshape(n, GDN_W), o_gla.reshape(n, GLA_W), o_fox.reshape(n, FOX_W), xf,
                         w_out_b, i, norm_post_mix[i], norm_pre_ffn[i])
        a = _ffn_up(h, w_up_b, conv_ffn, conv_ffn_bias, i, seq=t)
        if i + 1 < depth:
            xf, h = _ffn_down(a, w_down_b, i, xf, norm_post_ffn[i], norm_pre_mix[i + 1])
        else:
            xf = _ffn_down(a, w_down_b, i, xf, norm_post_ffn[i])
    return xf.reshape(b, t, d)
```

```python
import functools
import math

import jax
import jax.numpy as jnp
from jax import lax
from jax.experimental import pallas as pl
from jax.experimental.pallas import tpu as pltpu

F32 = jnp.float32
BF16 = jnp.bfloat16

D_MODEL = 2048
HEAD = 128
H_GDN, H_GLA, H_FOX = 6, 5, 5
GLA_DK = 64
GDN_CONV = 4
GLA_RANK = 16
GLA_NORMALIZER = 16.0
CHUNK = 64
SUPER = 256
D_FF = 4 * D_MODEL
EPS = 1e-6
LOG2E = math.log2(math.e)

GDN_W = H_GDN * HEAD
GLA_W = H_GLA * HEAD
FOX_W = H_FOX * HEAD
QKV_W = 3 * GDN_W
OFF_GLA = 0
OFF_FOX = 1920
OFF_GDN_Z = 3840
REST_W = 4608
LANE_GDN_B, LANE_GDN_A, LANE_GLA_LR, LANE_FOX_F = 0, 6, 12, 28

V7X_VMEM_LIMIT = 56 * 1024 * 1024


def _params(sem, vmem=V7X_VMEM_LIMIT, flags=None):
    return pltpu.CompilerParams(dimension_semantics=sem, vmem_limit_bytes=vmem, flags=flags)


def _dot(a, b):
    return jnp.dot(a, b, preferred_element_type=F32)


def _dot_nt(a, b):
    return lax.dot_general(a, b, (((1,), (1,)), ((), ())), preferred_element_type=F32)


def _dot_tn(a, b):
    return lax.dot_general(a, b, (((0,), (0,)), ((), ())), preferred_element_type=F32)


def _split3(x):
    x1 = x.astype(BF16)
    r = x - x1.astype(F32)
    x2 = r.astype(BF16)
    x3 = (r - x2.astype(F32)).astype(BF16)
    return x1, x2, x3


def _dot_exact_lhs(m, x):
    x1, x2, x3 = _split3(x)
    return _dot(m, x1) + _dot(m, x2) + _dot(m, x3)


def _sigmoid(x):
    return 1.0 / (1.0 + jnp.exp(-x))


def _softplus(x):
    return jnp.maximum(x, 0.0) + jnp.log1p(jnp.exp(-jnp.abs(x)))


def _log_sigmoid(x):
    return jnp.minimum(x, 0.0) - jnp.log1p(jnp.exp(-jnp.abs(x)))


def _rms(x, w):
    return x * lax.rsqrt(jnp.mean(x * x, axis=-1, keepdims=True) + EPS) * w


def _chunk_masks(n):
    row = lax.broadcasted_iota(jnp.int32, (n, n), 0)
    col = lax.broadcasted_iota(jnp.int32, (n, n), 1)
    same = (row // CHUNK) == (col // CHUNK)
    return same & (row >= col), same & (row > col)


def _rmsnorm_kernel(x_ref, w_ref, o_ref):
    o_ref[...] = _rms(x_ref[...], w_ref[...]).astype(o_ref.dtype)


def _rmsnorm(x, w, tm=512):
    n, d = x.shape
    return pl.pallas_call(
        _rmsnorm_kernel,
        grid=(n // tm,),
        in_specs=[pl.BlockSpec((tm, d), lambda i: (i, 0)), pl.BlockSpec((1, d), lambda i: (0, 0))],
        out_specs=pl.BlockSpec((tm, d), lambda i: (i, 0)),
        out_shape=jax.ShapeDtypeStruct((n, d), BF16),
        compiler_params=_params(("parallel",)),
        name="rmsnorm",
    )(x, w.reshape(1, d))


IN_ROWS = 256


def _in_proj_conv_kernel(a_ref, bt_ref, cw_ref, o_ref, tail_ref, *, tiles_per_seq):
    w = bt_ref[...]
    cw = cw_ref[...]

    @pl.when(pl.program_id(0) % tiles_per_seq == 0)
    def _():
        tail_ref[...] = jnp.zeros_like(tail_ref)

    tail = tail_ref[...]
    for r in range(a_ref.shape[0] // IN_ROWS):
        rows = slice(r * IN_ROWS, (r + 1) * IN_ROWS)
        u = _dot_nt(a_ref[rows, :], w)
        ue = jnp.concatenate([tail, u], axis=0)
        y = u * cw[GDN_CONV - 1:GDN_CONV, :]
        for t in range(1, GDN_CONV):
            y = y + pltpu.roll(ue, t, 0)[8:, :] * cw[GDN_CONV - 1 - t:GDN_CONV - t, :]
        o_ref[rows, :] = (y * _sigmoid(y)).astype(o_ref.dtype)
        tail = u[IN_ROWS - 8:, :]
    tail_ref[...] = tail


def _in_proj_conv(a, bt, conv_w, layer, seq):
    n, k = a.shape
    nc = bt.shape[1]
    tm = min(1024, seq)
    return pl.pallas_call(
        functools.partial(_in_proj_conv_kernel, tiles_per_seq=seq // tm),
        grid=(n // tm,),
        in_specs=[pl.BlockSpec((tm, k), lambda i: (i, 0)),
                  pl.BlockSpec((None, nc, k), lambda i: (layer, 0, 0)),
                  pl.BlockSpec((None, GDN_CONV, nc), lambda i: (layer, 0, 0))],
        out_specs=pl.BlockSpec((tm, nc), lambda i: (i, 0)),
        out_shape=jax.ShapeDtypeStruct((n, nc), BF16),
        scratch_shapes=[pltpu.VMEM((8, nc), F32)],
        compiler_params=_params(("arbitrary",)),
        name="in_proj_qkv",
    )(a, bt, conv_w)


def _mm_nt_kernel(a_ref, bt_ref, o_ref):
    o_ref[...] = _dot_nt(a_ref[...], bt_ref[...]).astype(o_ref.dtype)


def _matmul_nt(a, bt, layer, out_dtype, tm, tn, name):
    n, k = a.shape
    nc = bt.shape[1]
    return pl.pallas_call(
        _mm_nt_kernel,
        grid=(n // tm, nc // tn),
        in_specs=[pl.BlockSpec((tm, k), lambda i, j: (i, 0)),
                  pl.BlockSpec((None, tn, k), lambda i, j: (layer, j, 0))],
        out_specs=pl.BlockSpec((tm, tn), lambda i, j: (i, j)),
        out_shape=jax.ShapeDtypeStruct((n, nc), out_dtype),
        compiler_params=_params(("parallel", "parallel")),
        name=name,
    )(a, bt)


def _gdn_kernel(q_ref, k_ref, v_ref, z_ref, g_ref, alog_ref, dtb_ref, nw_ref, o_ref, state_ref, *, seq):
    heads = range(H_GDN)
    cols = [slice(h * HEAD, (h + 1) * HEAD) for h in heads]
    incl, strict = _chunk_masks(SUPER)
    tril_bd = incl.astype(BF16)
    neg_a = -jnp.exp(alog_ref[...])
    dtb = dtb_ref[...]
    nw = nw_ref[...]
    state_ref[...] = jnp.zeros_like(state_ref)

    def l2n(x):
        return x * lax.rsqrt(jnp.sum(x * x, axis=-1, keepdims=True) + EPS)

    def step(s, carry):
        r0 = pl.multiple_of(s * SUPER, SUPER)
        rows = pl.ds(r0, SUPER)
        gates = g_ref[rows, :]
        beta_all = _sigmoid(gates)
        g_all = neg_a * _softplus(gates + dtb)
        gc_all = _dot_exact_lhs(tril_bd, g_all)
        qn = [l2n(q_ref[rows, c].astype(F32)) * (HEAD ** -0.5) for c in cols]
        kn = [l2n(k_ref[rows, c].astype(F32)) for c in cols]
        vv = [v_ref[rows, c].astype(F32) for c in cols]

        beta = [beta_all[:, LANE_GDN_B + h:LANE_GDN_B + h + 1] for h in heads]
        gcb = [jnp.broadcast_to(gc_all[:, LANE_GDN_A + h:LANE_GDN_A + h + 1], (SUPER, SUPER)) for h in heads]
        gcl = [g[:, :HEAD] for g in gcb]
        decay = [jnp.where(incl, jnp.exp(g - g.T), 0.0) for g in gcb]
        kb = [kn[h] * beta[h] for h in heads]
        knb = [k.astype(BF16) for k in kn]
        sk = [_dot_nt(kb[h].astype(BF16), knb[h]) for h in heads]
        qk = [_dot_nt(qn[h].astype(BF16), knb[h]) for h in heads]
        a_mat = [jnp.where(strict, sk[h] * decay[h], 0.0) for h in heads]
        qk = [qk[h] * decay[h] for h in heads]
        rhs = [jnp.concatenate([vv[h] * beta[h], kb[h] * jnp.exp(gcl[h])], axis=1) for h in heads]
        p = [a.astype(BF16) for a in a_mat]
        x = [rhs[h] - _dot(p[h], rhs[h].astype(BF16)) for h in heads]
        for _ in range(5):
            p = [_dot(pi, pi).astype(BF16) for pi in p]
            x = [x[h] + _dot(p[h], x[h].astype(BF16)) for h in heads]
        u = [xi[:, :HEAD] for xi in x]
        w = [xi[:, HEAD:].astype(BF16) for xi in x]
        qg = [(qn[h] * jnp.exp(gcl[h])).astype(BF16) for h in heads]

        st = [state_ref[h] for h in heads]
        v_new = [[] for _ in heads]
        o_inter = [[] for _ in heads]
        for c in range(SUPER // CHUNK):
            rs = slice(c * CHUNK, (c + 1) * CHUNK)
            last = slice((c + 1) * CHUNK - 1, (c + 1) * CHUNK)
            stb = [t.astype(BF16) for t in st]
            vn = [u[h][rs] - _dot(w[h][rs], stb[h]) for h in heads]
            oi = [_dot(qg[h][rs], stb[h]) for h in heads]
            ke = [(kn[h][rs] * jnp.exp(gcl[h][last] - gcl[h][rs])).astype(BF16) for h in heads]
            st = [st[h] * jnp.exp(gcl[h][last]) + _dot_tn(ke[h], vn[h].astype(BF16)) for h in heads]
            for h in heads:
                v_new[h].append(vn[h])
                o_inter[h].append(oi[h])
        for h in heads:
            state_ref[h] = st[h]
        o = [jnp.concatenate(o_inter[h], axis=0)
             + _dot(qk[h].astype(BF16), jnp.concatenate(v_new[h], axis=0).astype(BF16)) for h in heads]
        for h in heads:
            z = z_ref[rows, cols[h]].astype(F32)
            o_ref[rows, cols[h]] = (_rms(o[h], nw) * (z * _sigmoid(z))).astype(o_ref.dtype)
        return carry

    lax.fori_loop(0, seq // SUPER, step, 0, unroll=4)


def _gdn_mixer(qkv, rest, gates, a_log, dt_bias, norm_w):
    b, t, _ = qkv.shape
    row = lambda v, off: jnp.zeros((1, HEAD), F32).at[0, off:off + v.shape[0]].set(v)
    act = lambda k: pl.BlockSpec((None, t, GDN_W), lambda bi: (bi, 0, k))
    vec = pl.BlockSpec((1, HEAD), lambda bi: (0, 0))
    return pl.pallas_call(
        functools.partial(_gdn_kernel, seq=t),
        grid=(b,),
        in_specs=[act(0), act(1), act(2), act(OFF_GDN_Z // GDN_W), pl.BlockSpec((None, t, HEAD), lambda bi: (bi, 0, 0)),
                  vec, vec, vec],
        out_specs=pl.BlockSpec((None, t, GDN_W), lambda bi: (bi, 0, 0)),
        out_shape=jax.ShapeDtypeStruct((b, t, GDN_W), BF16),
        scratch_shapes=[pltpu.VMEM((H_GDN, HEAD, HEAD), F32)],
        compiler_params=_params(("parallel",)),
        name="gdn_mixer",
    )(qkv, qkv, qkv, rest, gates, row(a_log, LANE_GDN_A), row(dt_bias, LANE_GDN_A), norm_w.reshape(1, HEAD))


def _gla_kernel(qk_ref, v_ref, go_ref, g_ref, wg_ref, bg_ref, nw_ref, o_ref, state_ref, *, seq):
    heads = range(H_GLA)
    cols = [slice(h * HEAD, (h + 1) * HEAD) for h in heads]
    incl, _ = _chunk_masks(SUPER)
    tril_bd = incl.astype(BF16)
    is_q = lax.broadcasted_iota(jnp.int32, (1, HEAD), 1) < GLA_DK
    sign = jnp.where(is_q, 1.0, -1.0)
    q_scale = jnp.where(is_q, GLA_DK ** -0.5, 1.0)
    nw = nw_ref[...]
    wg = wg_ref[...]
    wg_hi = wg.astype(BF16)
    wg_lo = (wg - wg_hi.astype(F32)).astype(BF16)
    bg = bg_ref[...]
    state_ref[...] = jnp.zeros_like(state_ref)

    def k_half(x):
        return jnp.where(is_q, pltpu.roll(x, GLA_DK, 1), 0.0).astype(BF16)

    def step(s, carry):
        r0 = pl.multiple_of(s * SUPER, SUPER)
        rows = pl.ds(r0, SUPER)
        lr = g_ref[rows, :]
        lr_hi = lr.astype(BF16)
        lr_lo = (lr - lr_hi.astype(F32)).astype(BF16)
        logits = _dot(lr_hi, wg_hi) + _dot(lr_hi, wg_lo) + _dot(lr_lo, wg_hi) + bg
        gk_all = _log_sigmoid(logits) / GLA_NORMALIZER
        bc = [_dot_exact_lhs(tril_bd, gk_all[:, c]) for c in cols]
        x = [qk_ref[rows, c].astype(F32) for c in cols]
        v = [v_ref[rows, c] for c in cols]
        xe = [x[h] * (jnp.exp(bc[h] * sign) * q_scale) for h in heads]
        q_dec = [jnp.where(is_q, xe[h], 0.0).astype(BF16) for h in heads]
        k_inv = [k_half(xe[h]) for h in heads]
        attn = [jnp.where(incl, _dot_nt(q_dec[h], k_inv[h]), 0.0).astype(BF16) for h in heads]
        o_intra = [_dot(attn[h], v[h]) for h in heads]
        st = [state_ref[h] for h in heads]
        o_inter = [[] for _ in heads]
        for c in range(SUPER // CHUNK):
            rs = slice(c * CHUNK, (c + 1) * CHUNK)
            last = slice((c + 1) * CHUNK - 1, (c + 1) * CHUNK)
            oi = [_dot_nt(q_dec[h][rs], st[h].astype(BF16)) for h in heads]
            k_end = [k_half(x[h][rs] * jnp.exp(bc[h][last] - bc[h][rs])) for h in heads]
            st = [st[h] * jnp.exp(bc[h][last]) + _dot_tn(v[h][rs], k_end[h]) for h in heads]
            for h in heads:
                o_inter[h].append(oi[h])
        for h in heads:
            state_ref[h] = st[h]
            o = o_intra[h] + jnp.concatenate(o_inter[h], axis=0)
            go = go_ref[rows, cols[h]].astype(F32)
            o_ref[rows, cols[h]] = (_rms(o, nw) * (go * _sigmoid(go))).astype(o_ref.dtype)
        return carry

    lax.fori_loop(0, seq // SUPER, step, 0, unroll=4)


def _gla_mixer(proj, gates, w_gate, b_gate, norm_w):
    b, t, _ = proj.shape
    twice = lambda m: jnp.tile(m.reshape(m.shape[0], H_GLA, 1, GLA_DK), (1, 1, 2, 1)).reshape(m.shape[0], GLA_W)
    wg = jnp.zeros((HEAD, GLA_W), F32).at[LANE_GLA_LR:LANE_GLA_LR + GLA_RANK].set(twice(w_gate))
    bg = twice(b_gate.reshape(1, -1))
    act = lambda k: pl.BlockSpec((None, t, GLA_W), lambda bi: (bi, 0, OFF_GLA // GLA_W + k))
    const = lambda shape: pl.BlockSpec(shape, lambda bi: (0, 0))
    return pl.pallas_call(
        functools.partial(_gla_kernel, seq=t),
        grid=(b,),
        in_specs=[act(0), act(1), act(2), pl.BlockSpec((None, t, HEAD), lambda bi: (bi, 0, 0)),
                  const((HEAD, GLA_W)), const((1, GLA_W)), const((1, HEAD))],
        out_specs=pl.BlockSpec((None, t, GLA_W), lambda bi: (bi, 0, 0)),
        out_shape=jax.ShapeDtypeStruct((b, t, GLA_W), BF16),
        scratch_shapes=[pltpu.VMEM((H_GLA, HEAD, HEAD), F32)],
        compiler_params=_params(("parallel",)),
        name="gla_mixer",
    )(proj, proj, proj, gates, wg, bg, norm_w.reshape(1, HEAD))


FOX_SPAN = 512
FOX_ROWS = 256


def _fox_kernel(q_ref, k_ref, v_ref, g_ref, fb_ref, o_ref, c_ref, ct_ref, vaug_ref, *, seq):
    heads = range(H_FOX)
    cols = [slice(h * HEAD, (h + 1) * HEAD) for h in heads]
    ones_col = (lax.broadcasted_iota(jnp.int32, (seq, HEAD), 1) == 0).astype(BF16)
    for h in heads:
        vaug_ref[:, 2 * h * HEAD:(2 * h + 1) * HEAD] = v_ref[:, cols[h]]
        vaug_ref[:, (2 * h + 1) * HEAD:(2 * h + 2) * HEAD] = ones_col
    lanes = [slice(LANE_FOX_F + h, LANE_FOX_F + h + 1) for h in heads]
    row = lax.broadcasted_iota(jnp.int32, (SUPER, SUPER), 0)
    col = lax.broadcasted_iota(jnp.int32, (SUPER, SUPER), 1)
    tril = (row >= col).astype(BF16)
    fb = fb_ref[...]
    run = jnp.zeros((1, HEAD), F32)
    for i in range(seq // SUPER):
        rs = slice(i * SUPER, (i + 1) * SUPER)
        cb = _dot_exact_lhs(tril, _log_sigmoid(g_ref[rs, :] + fb)) + run
        c_ref[rs, :] = cb
        ct_ref[:, rs] = cb.T
        run = cb[SUPER - 1:SUPER, :]

    qrow = lax.broadcasted_iota(jnp.int32, (FOX_ROWS, FOX_SPAN), 0)
    kcol = lax.broadcasted_iota(jnp.int32, (FOX_ROWS, FOX_SPAN), 1)
    scale = HEAD ** -0.5

    for span in range(seq // FOX_SPAN):
        hi = (span + 1) * FOX_SPAN
        lo = hi - FOX_SPAN

        def q_step(sb, carry, hi=hi, lo=lo):
            r0 = pl.multiple_of(lo + sb * FOX_ROWS, FOX_ROWS)
            cq_all = c_ref[pl.ds(r0, FOX_ROWS), :] * LOG2E
            causal = qrow + sb * FOX_ROWS >= kcol
            s = [_dot_nt(q_ref[pl.ds(r0, FOX_ROWS), cols[h]], k_ref[0:hi, cols[h]]) for h in heads]
            s = [s[h] * (scale * LOG2E) + (cq_all[:, lanes[h]] - ct_ref[lanes[h], 0:hi] * LOG2E) for h in heads]
            tail = [jnp.where(causal, sh[:, lo:], -jnp.inf) for sh in s]
            s = [jnp.concatenate([s[h][:, :lo], tail[h]], axis=1) for h in heads] if lo else tail
            p = [jnp.exp2(sh - jnp.max(sh, axis=1, keepdims=True)).astype(BF16) for sh in s]
            pv = [_dot(p[h], vaug_ref[0:hi, 2 * h * HEAD:(2 * h + 2) * HEAD]) for h in heads]
            for h in heads:
                o_ref[pl.ds(r0, FOX_ROWS), cols[h]] = (pv[h][:, :HEAD] / pv[h][:, HEAD:HEAD + 1]).astype(o_ref.dtype)
            return carry

        lax.fori_loop(0, FOX_SPAN // FOX_ROWS, q_step, 0, unroll=True)


def _fox_mixer(proj, gates, f_bias):
    b, t, _ = proj.shape
    fb = jnp.zeros((1, HEAD), F32).at[0, LANE_FOX_F:LANE_FOX_F + H_FOX].set(f_bias)
    act = lambda k: pl.BlockSpec((None, t, FOX_W), lambda bi: (bi, 0, OFF_FOX // FOX_W + k))
    return pl.pallas_call(
        functools.partial(_fox_kernel, seq=t),
        grid=(b,),
        in_specs=[act(0), act(1), act(2), pl.BlockSpec((None, t, HEAD), lambda bi: (bi, 0, 0)),
                  pl.BlockSpec((1, HEAD), lambda bi: (0, 0))],
        out_specs=pl.BlockSpec((None, t, FOX_W), lambda bi: (bi, 0, 0)),
        out_shape=jax.ShapeDtypeStruct((b, t, FOX_W), BF16),
        scratch_shapes=[pltpu.VMEM((t, HEAD), F32), pltpu.VMEM((HEAD, t), F32), pltpu.VMEM((t, 2 * FOX_W), BF16)],
        compiler_params=_params(("parallel",)),
        name="fox_mixer",
    )(proj, proj, proj, gates, fb)


OUT_ROWS = 128


def _outproj_kernel(og_ref, ol_ref, of_ref, x_ref, w_ref, npost_ref, npre_ref, wup_ref, x1_ref, h_ref, wupb_ref,
                    cat_ref):
    wupb_ref[...] = wup_ref[...].astype(wupb_ref.dtype)
    cat_ref[:, :GDN_W] = og_ref[...]
    cat_ref[:, GDN_W:GDN_W + GLA_W] = ol_ref[...]
    cat_ref[:, GDN_W + GLA_W:] = of_ref[...]
    w = w_ref[...]
    for r in range(cat_ref.shape[0] // OUT_ROWS):
        rows = slice(r * OUT_ROWS, (r + 1) * OUT_ROWS)
        x1 = x_ref[rows, :] + _rms(_dot(cat_ref[rows, :], w), npost_ref[...])
        x1_ref[rows, :] = x1
        h_ref[rows, :] = _rms(x1, npre_ref[...]).astype(h_ref.dtype)


def _outproj(o_gdn, o_gla, o_fox, x, w_out, layer, n_post, n_pre_next, w_up, tm=512):
    n, d = x.shape
    steps = n // tm
    wrows, wcols = w_up.shape[1] // steps, w_up.shape[2]
    rowblk = lambda wd: pl.BlockSpec((tm, wd), lambda i: (i, 0))
    const = lambda shape: pl.BlockSpec(shape, lambda i: (0, 0))
    return pl.pallas_call(
        _outproj_kernel,
        grid=(steps,),
        in_specs=[rowblk(GDN_W), rowblk(GLA_W), rowblk(FOX_W), rowblk(d),
                  pl.BlockSpec((None, d, d), lambda i: (layer, 0, 0), pipeline_mode=pl.Buffered(1)),
                  const((1, d)), const((1, d)), pl.BlockSpec((None, wrows, wcols), lambda i: (layer, i, 0))],
        out_specs=[rowblk(d), rowblk(d), pl.BlockSpec((wrows, wcols), lambda i: (i, 0))],
        out_shape=[jax.ShapeDtypeStruct((n, d), F32), jax.ShapeDtypeStruct((n, d), BF16),
                   jax.ShapeDtypeStruct(w_up.shape[1:], BF16)],
        scratch_shapes=[pltpu.VMEM((tm, d), BF16)],
        compiler_params=_params(("parallel",)),
        name="out_proj",
    )(o_gdn, o_gla, o_fox, x, w_out, n_post.reshape(1, d), n_pre_next.reshape(1, d), w_up)


def _gelu_tanh(x):
    k = 2.0 * math.sqrt(2.0 / math.pi)
    return x / (1.0 + jnp.exp(x * ((-k * 0.044715) * (x * x) - k)))


FFN_ROWS = 256


def _ffn_up_kernel(h_ref, wg_ref, wv_ref, cg_ref, cv_ref, bg_ref, bv_ref, o_ref):
    seq, tf = o_ref.shape
    wg = wg_ref[...]
    wv = wv_ref[...]

    def conv(u, tail, c_ref, b_ref):
        c = c_ref[...]
        ue = jnp.concatenate([tail, u], axis=0)
        return (u * c[2:3, :] + pltpu.roll(ue, 1, 0)[8:, :] * c[1:2, :] + pltpu.roll(ue, 2, 0)[8:, :] * c[0:1, :]
                + b_ref[...])

    tail_g = jnp.zeros((8, tf), F32)
    tail_v = jnp.zeros((8, tf), F32)
    for r in range(seq // FFN_ROWS):
        rows = slice(r * FFN_ROWS, (r + 1) * FFN_ROWS)
        h = h_ref[rows, :]
        ug = _dot(h, wg)
        uv = _dot(h, wv)
        o_ref[rows, :] = (_gelu_tanh(conv(ug, tail_g, cg_ref, bg_ref)) * conv(uv, tail_v, cv_ref, bv_ref)).astype(o_ref.dtype)
        tail_g, tail_v = ug[FFN_ROWS - 8:, :], uv[FFN_ROWS - 8:, :]


def _ffn_up(h, w_up, conv_w, conv_b, layer, seq, tf=1024):
    n, d = h.shape
    nf = D_FF // tf
    conv_b = conv_b.reshape(conv_b.shape[0], 1, -1)
    wspec = lambda off: pl.BlockSpec((d, tf), lambda i, j: (0, off + j))
    cspec = lambda rows, off: pl.BlockSpec((None, rows, tf), lambda i, j: (layer, 0, off + j))
    return pl.pallas_call(
        _ffn_up_kernel,
        grid=(n // seq, nf),
        in_specs=[pl.BlockSpec((seq, d), lambda i, j: (i, 0)), wspec(0), wspec(nf),
                  cspec(3, 0), cspec(3, nf), cspec(1, 0), cspec(1, nf)],
        out_specs=pl.BlockSpec((seq, tf), lambda i, j: (i, j)),
        out_shape=jax.ShapeDtypeStruct((n, D_FF), BF16),
        compiler_params=_params(("parallel", "parallel")),
        name="ffn_up",
    )(h, w_up, w_up, conv_w, conv_w, conv_b, conv_b)


def _ffn_down_kernel(a_ref, w_ref, x_ref, npost_ref, *rest):
    x2 = x_ref[...] + _rms(_dot(a_ref[...], w_ref[...]), npost_ref[...])
    if len(rest) == 1:
        rest[0][...] = x2
    else:
        nnext_ref, x2_ref, h_ref = rest
        x2_ref[...] = x2
        h_ref[...] = _rms(x2, nnext_ref[...]).astype(h_ref.dtype)


def _ffn_down(a, w_down, layer, x, n_post, n_next=None, tm=256):
    n, d = x.shape
    kdim = a.shape[1]
    const = pl.BlockSpec((1, d), lambda i: (0, 0))
    rowblk = pl.BlockSpec((tm, d), lambda i: (i, 0))
    in_specs = [pl.BlockSpec((tm, kdim), lambda i: (i, 0)),
                pl.BlockSpec((None, kdim, d), lambda i: (layer, 0, 0), pipeline_mode=pl.Buffered(1)), rowblk, const]
    args = [a, w_down, x, n_post.reshape(1, d)]
    x_out = jax.ShapeDtypeStruct((n, d), F32)
    if n_next is None:
        out_specs, out_shape = rowblk, x_out
    else:
        in_specs.append(const)
        args.append(n_next.reshape(1, d))
        out_specs, out_shape = [rowblk, rowblk], [x_out, jax.ShapeDtypeStruct((n, d), BF16)]
    return pl.pallas_call(
        _ffn_down_kernel,
        grid=(n // tm,),
        in_specs=in_specs,
        out_specs=out_specs,
        out_shape=out_shape,
        compiler_params=_params(("parallel",)),
        name="ffn_down",
    )(*args)


def _pack_in_proj(w_in):
    wt = jnp.swapaxes(w_in, 1, 2)
    depth, _, d = wt.shape
    o = 0
    segs = {}
    for name, width in (("gdn", 4 * GDN_W), ("gdn_b", H_GDN), ("gdn_a", H_GDN), ("gla_q", H_GLA * GLA_DK),
                        ("gla_k", H_GLA * GLA_DK), ("gla_vg", 2 * GLA_W), ("gla_lr", GLA_RANK),
                        ("fox", 3 * FOX_W), ("fox_f", H_FOX)):
        segs[name] = wt[:, o:o + width]
        o += width
    per_head = lambda m: m.reshape(depth, H_GLA, GLA_DK, d)
    gla_qk = jnp.concatenate([per_head(segs["gla_q"]), per_head(segs["gla_k"])], axis=2).reshape(depth, GLA_W, d)
    zeros = lambda n: jnp.zeros((depth, n, d), wt.dtype)
    qkv = segs["gdn"][:, :QKV_W]
    rest = jnp.concatenate([gla_qk, segs["gla_vg"], segs["fox"], segs["gdn"][:, QKV_W:]], axis=1)
    small = jnp.concatenate([segs["gdn_b"], segs["gdn_a"], segs["gla_lr"], segs["fox_f"],
                             zeros(HEAD - 2 * H_GDN - GLA_RANK - H_FOX)], axis=1)
    return qkv.astype(BF16), rest.astype(BF16), small.astype(BF16)


def kernel(x, w_in, conv_gdn, gdn_a_log, gdn_dt_bias, gdn_norm, gla_w_gate, gla_b_gate, gla_norm, fox_f_bias, w_out, norm_pre_mix, norm_post_mix, norm_pre_ffn, norm_post_ffn, w_up, conv_ffn, conv_ffn_bias, w_down):
    b, t, d = x.shape
    n = b * t
    depth = w_in.shape[0]
    xf = x.reshape(n, d)
    w_qkv, w_rest, w_small = _pack_in_proj(w_in)
    w_out_b, w_down_b = w_out.astype(BF16), w_down.astype(BF16)
    h = _rmsnorm(xf, norm_pre_mix[0])
    for i in range(depth):
        qkv = _in_proj_conv(h, w_qkv, conv_gdn, i, seq=t).reshape(b, t, QKV_W)
        rest = _matmul_nt(h, w_rest, i, BF16, tm=1024, tn=REST_W // 3, name="in_proj").reshape(b, t, REST_W)
        gates = _matmul_nt(h, w_small, i, F32, tm=1024, tn=HEAD, name="in_proj_gates").reshape(b, t, HEAD)
        o_gdn = _gdn_mixer(qkv, rest, gates, gdn_a_log[i], gdn_dt_bias[i], gdn_norm[i])
        o_gla = _gla_mixer(rest, gates, gla_w_gate[i], gla_b_gate[i], gla_norm[i])
        o_fox = _fox_mixer(rest, gates, fox_f_bias[i])
        xf, h, w_up_b = _outproj(o_gdn.reshape(n, GDN_W), o_gla.reshape(n, GLA_W), o_fox.reshape(n, FOX_W), xf,
                                 w_out_b, i, norm_post_mix[i], norm_pre_ffn[i], w_up)
        a = _ffn_up(h, w_up_b, conv_ffn, conv_ffn_bias, i, seq=t)
        if i + 1 < depth:
            xf, h = _ffn_down(a, w_down_b, i, xf, norm_post_ffn[i], norm_pre_mix[i + 1])
        else:
            xf = _ffn_down(a, w_down_b, i, xf, norm_post_ffn[i])
    return xf.reshape(b, t, d)
```

```python
import functools
import math

import jax
import jax.numpy as jnp
from jax import lax
from jax.experimental import pallas as pl
from jax.experimental.pallas import tpu as pltpu

F32 = jnp.float32
BF16 = jnp.bfloat16

D_MODEL = 2048
HEAD = 128
H_GDN, H_GLA, H_FOX = 6, 5, 5
GLA_DK = 64
GDN_CONV = 4
GLA_RANK = 16
GLA_NORMALIZER = 16.0
CHUNK = 64
SUPER = 256
D_FF = 4 * D_MODEL
EPS = 1e-6
LOG2E = math.log2(math.e)

GDN_W = H_GDN * HEAD
GLA_W = H_GLA * HEAD
FOX_W = H_FOX * HEAD
QKV_W = 3 * GDN_W
OFF_GLA = 0
OFF_FOX = 1920
OFF_GDN_Z = 3840
REST_W = 4608
LANE_GDN_B, LANE_GDN_A, LANE_GLA_LR, LANE_FOX_F = 0, 6, 12, 28

V7X_VMEM_LIMIT = 56 * 1024 * 1024


def _params(sem, vmem=V7X_VMEM_LIMIT, flags=None):
    return pltpu.CompilerParams(dimension_semantics=sem, vmem_limit_bytes=vmem, flags=flags)


def _dot(a, b):
    return jnp.dot(a, b, preferred_element_type=F32)


def _dot_nt(a, b):
    return lax.dot_general(a, b, (((1,), (1,)), ((), ())), preferred_element_type=F32)


def _dot_tn(a, b):
    return lax.dot_general(a, b, (((0,), (0,)), ((), ())), preferred_element_type=F32)


def _split3(x):
    x1 = x.astype(BF16)
    r = x - x1.astype(F32)
    x2 = r.astype(BF16)
    x3 = (r - x2.astype(F32)).astype(BF16)
    return x1, x2, x3


def _dot_exact_lhs(m, x):
    x1, x2, x3 = _split3(x)
    return _dot(m, x1) + _dot(m, x2) + _dot(m, x3)


def _sigmoid(x):
    return 1.0 / (1.0 + jnp.exp(-x))


def _softplus(x):
    return jnp.maximum(x, 0.0) + jnp.log1p(jnp.exp(-jnp.abs(x)))


def _log_sigmoid(x):
    return jnp.minimum(x, 0.0) - jnp.log1p(jnp.exp(-jnp.abs(x)))


def _rms(x, w):
    return x * lax.rsqrt(jnp.mean(x * x, axis=-1, keepdims=True) + EPS) * w


def _chunk_masks(n):
    row = lax.broadcasted_iota(jnp.int32, (n, n), 0)
    col = lax.broadcasted_iota(jnp.int32, (n, n), 1)
    same = (row // CHUNK) == (col // CHUNK)
    return same & (row >= col), same & (row > col)


def _rmsnorm_kernel(x_ref, w_ref, o_ref):
    o_ref[...] = _rms(x_ref[...], w_ref[...]).astype(o_ref.dtype)


def _rmsnorm(x, w, tm=512):
    n, d = x.shape
    return pl.pallas_call(
        _rmsnorm_kernel,
        grid=(n // tm,),
        in_specs=[pl.BlockSpec((tm, d), lambda i: (i, 0)), pl.BlockSpec((1, d), lambda i: (0, 0))],
        out_specs=pl.BlockSpec((tm, d), lambda i: (i, 0)),
        out_shape=jax.ShapeDtypeStruct((n, d), BF16),
        compiler_params=_params(("parallel",)),
        name="rmsnorm",
    )(x, w.reshape(1, d))


IN_ROWS = 256


def _in_proj_conv_kernel(a_ref, bt_ref, cw_ref, wd_ref, o_ref, wdb_ref, tail_ref, *, tiles_per_seq):
    wdb_ref[...] = wd_ref[...].astype(wdb_ref.dtype)
    w = bt_ref[...]
    cw = cw_ref[...]

    @pl.when(pl.program_id(0) % tiles_per_seq == 0)
    def _():
        tail_ref[...] = jnp.zeros_like(tail_ref)

    tail = tail_ref[...]
    for r in range(a_ref.shape[0] // IN_ROWS):
        rows = slice(r * IN_ROWS, (r + 1) * IN_ROWS)
        u = _dot_nt(a_ref[rows, :], w)
        ue = jnp.concatenate([tail, u], axis=0)
        y = u * cw[GDN_CONV - 1:GDN_CONV, :]
        for t in range(1, GDN_CONV):
            y = y + pltpu.roll(ue, t, 0)[8:, :] * cw[GDN_CONV - 1 - t:GDN_CONV - t, :]
        o_ref[rows, :] = (y * _sigmoid(y)).astype(o_ref.dtype)
        tail = u[IN_ROWS - 8:, :]
    tail_ref[...] = tail


def _in_proj_conv(a, bt, conv_w, layer, seq, w_down):
    n, k = a.shape
    nc = bt.shape[1]
    tm = min(1024, seq)
    steps = n // tm
    wrows, wcols = w_down.shape[1] // steps, w_down.shape[2]
    return pl.pallas_call(
        functools.partial(_in_proj_conv_kernel, tiles_per_seq=seq // tm),
        grid=(steps,),
        in_specs=[pl.BlockSpec((tm, k), lambda i: (i, 0)),
                  pl.BlockSpec((None, nc, k), lambda i: (layer, 0, 0), pipeline_mode=pl.Buffered(1)),
                  pl.BlockSpec((None, GDN_CONV, nc), lambda i: (layer, 0, 0)),
                  pl.BlockSpec((None, wrows, wcols), lambda i: (layer, i, 0))],
        out_specs=[pl.BlockSpec((tm, nc), lambda i: (i, 0)), pl.BlockSpec((wrows, wcols), lambda i: (i, 0))],
        out_shape=[jax.ShapeDtypeStruct((n, nc), BF16), jax.ShapeDtypeStruct(w_down.shape[1:], BF16)],
        scratch_shapes=[pltpu.VMEM((8, nc), F32)],
        compiler_params=_params(("arbitrary",)),
        name="in_proj_qkv",
    )(a, bt, conv_w, w_down)


def _mm_nt_kernel(a_ref, bt_ref, o_ref):
    o_ref[...] = _dot_nt(a_ref[...], bt_ref[...]).astype(o_ref.dtype)


def _matmul_nt(a, bt, layer, out_dtype, tm, tn, name):
    n, k = a.shape
    nc = bt.shape[1]
    return pl.pallas_call(
        _mm_nt_kernel,
        grid=(n // tm, nc // tn),
        in_specs=[pl.BlockSpec((tm, k), lambda i, j: (i, 0)),
                  pl.BlockSpec((None, tn, k), lambda i, j: (layer, j, 0))],
        out_specs=pl.BlockSpec((tm, tn), lambda i, j: (i, j)),
        out_shape=jax.ShapeDtypeStruct((n, nc), out_dtype),
        compiler_params=_params(("parallel", "parallel")),
        name=name,
    )(a, bt)


def _gdn_kernel(q_ref, k_ref, v_ref, z_ref, g_ref, alog_ref, dtb_ref, nw_ref, o_ref, state_ref, *, seq):
    heads = range(H_GDN)
    cols = [slice(h * HEAD, (h + 1) * HEAD) for h in heads]
    incl, strict = _chunk_masks(SUPER)
    tril_bd = incl.astype(BF16)
    neg_a = -jnp.exp(alog_ref[...])
    dtb = dtb_ref[...]
    nw = nw_ref[...]
    state_ref[...] = jnp.zeros_like(state_ref)

    def l2n(x):
        return x * lax.rsqrt(jnp.sum(x * x, axis=-1, keepdims=True) + EPS)

    def step(s, carry):
        r0 = pl.multiple_of(s * SUPER, SUPER)
        rows = pl.ds(r0, SUPER)
        gates = g_ref[rows, :]
        beta_all = _sigmoid(gates)
        g_all = neg_a * _softplus(gates + dtb)
        gc_all = _dot_exact_lhs(tril_bd, g_all)
        qn = [l2n(q_ref[rows, c].astype(F32)) * (HEAD ** -0.5) for c in cols]
        kn = [l2n(k_ref[rows, c].astype(F32)) for c in cols]
        vv = [v_ref[rows, c].astype(F32) for c in cols]

        beta = [beta_all[:, LANE_GDN_B + h:LANE_GDN_B + h + 1] for h in heads]
        gcb = [jnp.broadcast_to(gc_all[:, LANE_GDN_A + h:LANE_GDN_A + h + 1], (SUPER, SUPER)) for h in heads]
        gcl = [g[:, :HEAD] for g in gcb]
        decay = [jnp.where(incl, jnp.exp(g - g.T), 0.0) for g in gcb]
        kb = [kn[h] * beta[h] for h in heads]
        knb = [k.astype(BF16) for k in kn]
        sk = [_dot_nt(kb[h].astype(BF16), knb[h]) for h in heads]
        qk = [_dot_nt(qn[h].astype(BF16), knb[h]) for h in heads]
        a_mat = [jnp.where(strict, sk[h] * decay[h], 0.0) for h in heads]
        qk = [qk[h] * decay[h] for h in heads]
        rhs = [jnp.concatenate([vv[h] * beta[h], kb[h] * jnp.exp(gcl[h])], axis=1) for h in heads]
        p = [a.astype(BF16) for a in a_mat]
        x = [rhs[h] - _dot(p[h], rhs[h].astype(BF16)) for h in heads]
        for _ in range(5):
            p = [_dot(pi, pi).astype(BF16) for pi in p]
            x = [x[h] + _dot(p[h], x[h].astype(BF16)) for h in heads]
        u = [xi[:, :HEAD] for xi in x]
        w = [xi[:, HEAD:].astype(BF16) for xi in x]
        qg = [(qn[h] * jnp.exp(gcl[h])).astype(BF16) for h in heads]

        st = [state_ref[h] for h in heads]
        v_new = [[] for _ in heads]
        o_inter = [[] for _ in heads]
        for c in range(SUPER // CHUNK):
            rs = slice(c * CHUNK, (c + 1) * CHUNK)
            last = slice((c + 1) * CHUNK - 1, (c + 1) * CHUNK)
            stb = [t.astype(BF16) for t in st]
            vn = [u[h][rs] - _dot(w[h][rs], stb[h]) for h in heads]
            oi = [_dot(qg[h][rs], stb[h]) for h in heads]
            ke = [(kn[h][rs] * jnp.exp(gcl[h][last] - gcl[h][rs])).astype(BF16) for h in heads]
            st = [st[h] * jnp.exp(gcl[h][last]) + _dot_tn(ke[h], vn[h].astype(BF16)) for h in heads]
            for h in heads:
                v_new[h].append(vn[h])
                o_inter[h].append(oi[h])
        for h in heads:
            state_ref[h] = st[h]
        o = [jnp.concatenate(o_inter[h], axis=0)
             + _dot(qk[h].astype(BF16), jnp.concatenate(v_new[h], axis=0).astype(BF16)) for h in heads]
        for h in heads:
            z = z_ref[rows, cols[h]].astype(F32)
            o_ref[rows, cols[h]] = (_rms(o[h], nw) * (z * _sigmoid(z))).astype(o_ref.dtype)
        return carry

    lax.fori_loop(0, seq // SUPER, step, 0, unroll=4)


def _gdn_mixer(qkv, rest, gates, a_log, dt_bias, norm_w):
    b, t, _ = qkv.shape
    row = lambda v, off: jnp.zeros((1, HEAD), F32).at[0, off:off + v.shape[0]].set(v)
    act = lambda k: pl.BlockSpec((None, t, GDN_W), lambda bi: (bi, 0, k))
    vec = pl.BlockSpec((1, HEAD), lambda bi: (0, 0))
    return pl.pallas_call(
        functools.partial(_gdn_kernel, seq=t),
        grid=(b,),
        in_specs=[act(0), act(1), act(2), act(OFF_GDN_Z // GDN_W), pl.BlockSpec((None, t, HEAD), lambda bi: (bi, 0, 0)),
                  vec, vec, vec],
        out_specs=pl.BlockSpec((None, t, GDN_W), lambda bi: (bi, 0, 0)),
        out_shape=jax.ShapeDtypeStruct((b, t, GDN_W), BF16),
        scratch_shapes=[pltpu.VMEM((H_GDN, HEAD, HEAD), F32)],
        compiler_params=_params(("parallel",)),
        name="gdn_mixer",
    )(qkv, qkv, qkv, rest, gates, row(a_log, LANE_GDN_A), row(dt_bias, LANE_GDN_A), norm_w.reshape(1, HEAD))


def _gla_kernel(qk_ref, v_ref, go_ref, g_ref, wg_ref, bg_ref, nw_ref, o_ref, state_ref, *, seq):
    heads = range(H_GLA)
    cols = [slice(h * HEAD, (h + 1) * HEAD) for h in heads]
    incl, _ = _chunk_masks(SUPER)
    tril_bd = incl.astype(BF16)
    is_q = lax.broadcasted_iota(jnp.int32, (1, HEAD), 1) < GLA_DK
    sign = jnp.where(is_q, 1.0, -1.0)
    q_scale = jnp.where(is_q, GLA_DK ** -0.5, 1.0)
    nw = nw_ref[...]
    wg = wg_ref[...]
    wg_hi = wg.astype(BF16)
    wg_lo = (wg - wg_hi.astype(F32)).astype(BF16)
    bg = bg_ref[...]
    state_ref[...] = jnp.zeros_like(state_ref)

    def k_half(x):
        return jnp.where(is_q, pltpu.roll(x, GLA_DK, 1), 0.0).astype(BF16)

    def step(s, carry):
        r0 = pl.multiple_of(s * SUPER, SUPER)
        rows = pl.ds(r0, SUPER)
        lr = g_ref[rows, :]
        lr_hi = lr.astype(BF16)
        lr_lo = (lr - lr_hi.astype(F32)).astype(BF16)
        logits = _dot(lr_hi, wg_hi) + _dot(lr_hi, wg_lo) + _dot(lr_lo, wg_hi) + bg
        gk_all = _log_sigmoid(logits) / GLA_NORMALIZER
        bc = [_dot_exact_lhs(tril_bd, gk_all[:, c]) for c in cols]
        x = [qk_ref[rows, c].astype(F32) for c in cols]
        v = [v_ref[rows, c] for c in cols]
        xe = [x[h] * (jnp.exp(bc[h] * sign) * q_scale) for h in heads]
        q_dec = [jnp.where(is_q, xe[h], 0.0).astype(BF16) for h in heads]
        k_inv = [k_half(xe[h]) for h in heads]
        attn = [jnp.where(incl, _dot_nt(q_dec[h], k_inv[h]), 0.0).astype(BF16) for h in heads]
        o_intra = [_dot(attn[h], v[h]) for h in heads]
        st = [state_ref[h] for h in heads]
        o_inter = [[] for _ in heads]
        for c in range(SUPER // CHUNK):
            rs = slice(c * CHUNK, (c + 1) * CHUNK)
            last = slice((c + 1) * CHUNK - 1, (c + 1) * CHUNK)
            oi = [_dot_nt(q_dec[h][rs], st[h].astype(BF16)) for h in heads]
            k_end = [k_half(x[h][rs] * jnp.exp(bc[h][last] - bc[h][rs])) for h in heads]
            st = [st[h] * jnp.exp(bc[h][last]) + _dot_tn(v[h][rs], k_end[h]) for h in heads]
            for h in heads:
                o_inter[h].append(oi[h])
        for h in heads:
            state_ref[h] = st[h]
            o = o_intra[h] + jnp.concatenate(o_inter[h], axis=0)
            go = go_ref[rows, cols[h]].astype(F32)
            o_ref[rows, cols[h]] = (_rms(o, nw) * (go * _sigmoid(go))).astype(o_ref.dtype)
        return carry

    lax.fori_loop(0, seq // SUPER, step, 0, unroll=4)


def _gla_mixer(proj, gates, w_gate, b_gate, norm_w):
    b, t, _ = proj.shape
    twice = lambda m: jnp.tile(m.reshape(m.shape[0], H_GLA, 1, GLA_DK), (1, 1, 2, 1)).reshape(m.shape[0], GLA_W)
    wg = jnp.zeros((HEAD, GLA_W), F32).at[LANE_GLA_LR:LANE_GLA_LR + GLA_RANK].set(twice(w_gate))
    bg = twice(b_gate.reshape(1, -1))
    act = lambda k: pl.BlockSpec((None, t, GLA_W), lambda bi: (bi, 0, OFF_GLA // GLA_W + k))
    const = lambda shape: pl.BlockSpec(shape, lambda bi: (0, 0))
    return pl.pallas_call(
        functools.partial(_gla_kernel, seq=t),
        grid=(b,),
        in_specs=[act(0), act(1), act(2), pl.BlockSpec((None, t, HEAD), lambda bi: (bi, 0, 0)),
                  const((HEAD, GLA_W)), const((1, GLA_W)), const((1, HEAD))],
        out_specs=pl.BlockSpec((None, t, GLA_W), lambda bi: (bi, 0, 0)),
        out_shape=jax.ShapeDtypeStruct((b, t, GLA_W), BF16),
        scratch_shapes=[pltpu.VMEM((H_GLA, HEAD, HEAD), F32)],
        compiler_params=_params(("parallel",)),
        name="gla_mixer",
    )(proj, proj, proj, gates, wg, bg, norm_w.reshape(1, HEAD))


FOX_SPAN = 512
FOX_ROWS = 256


def _fox_kernel(q_ref, k_ref, v_ref, g_ref, fb_ref, o_ref, c_ref, ct_ref, vaug_ref, *, seq):
    heads = range(H_FOX)
    cols = [slice(h * HEAD, (h + 1) * HEAD) for h in heads]
    ones_col = (lax.broadcasted_iota(jnp.int32, (seq, HEAD), 1) == 0).astype(BF16)
    for h in heads:
        vaug_ref[:, 2 * h * HEAD:(2 * h + 1) * HEAD] = v_ref[:, cols[h]]
        vaug_ref[:, (2 * h + 1) * HEAD:(2 * h + 2) * HEAD] = ones_col
    lanes = [slice(LANE_FOX_F + h, LANE_FOX_F + h + 1) for h in heads]
    row = lax.broadcasted_iota(jnp.int32, (SUPER, SUPER), 0)
    col = lax.broadcasted_iota(jnp.int32, (SUPER, SUPER), 1)
    tril = (row >= col).astype(BF16)
    fb = fb_ref[...]
    run = jnp.zeros((1, HEAD), F32)
    for i in range(seq // SUPER):
        rs = slice(i * SUPER, (i + 1) * SUPER)
        cb = _dot_exact_lhs(tril, _log_sigmoid(g_ref[rs, :] + fb)) + run
        c_ref[rs, :] = cb
        ct_ref[:, rs] = cb.T
        run = cb[SUPER - 1:SUPER, :]

    qrow = lax.broadcasted_iota(jnp.int32, (FOX_ROWS, FOX_SPAN), 0)
    kcol = lax.broadcasted_iota(jnp.int32, (FOX_ROWS, FOX_SPAN), 1)
    scale = HEAD ** -0.5

    for span in range(seq // FOX_SPAN):
        hi = (span + 1) * FOX_SPAN
        lo = hi - FOX_SPAN

        def q_step(sb, carry, hi=hi, lo=lo):
            r0 = pl.multiple_of(lo + sb * FOX_ROWS, FOX_ROWS)
            cq_all = c_ref[pl.ds(r0, FOX_ROWS), :] * LOG2E
            causal = qrow + sb * FOX_ROWS >= kcol
            s = [_dot_nt(q_ref[pl.ds(r0, FOX_ROWS), cols[h]], k_ref[0:hi, cols[h]]) for h in heads]
            s = [s[h] * (scale * LOG2E) + (cq_all[:, lanes[h]] - ct_ref[lanes[h], 0:hi] * LOG2E) for h in heads]
            tail = [jnp.where(causal, sh[:, lo:], -jnp.inf) for sh in s]
            s = [jnp.concatenate([s[h][:, :lo], tail[h]], axis=1) for h in heads] if lo else tail
            p = [jnp.exp2(sh - jnp.max(sh, axis=1, keepdims=True)).astype(BF16) for sh in s]
            pv = [_dot(p[h], vaug_ref[0:hi, 2 * h * HEAD:(2 * h + 2) * HEAD]) for h in heads]
            for h in heads:
                o_ref[pl.ds(r0, FOX_ROWS), cols[h]] = (pv[h][:, :HEAD] / pv[h][:, HEAD:HEAD + 1]).astype(o_ref.dtype)
            return carry

        lax.fori_loop(0, FOX_SPAN // FOX_ROWS, q_step, 0, unroll=True)


def _fox_mixer(proj, gates, f_bias):
    b, t, _ = proj.shape
    fb = jnp.zeros((1, HEAD), F32).at[0, LANE_FOX_F:LANE_FOX_F + H_FOX].set(f_bias)
    act = lambda k: pl.BlockSpec((None, t, FOX_W), lambda bi: (bi, 0, OFF_FOX // FOX_W + k))
    return pl.pallas_call(
        functools.partial(_fox_kernel, seq=t),
        grid=(b,),
        in_specs=[act(0), act(1), act(2), pl.BlockSpec((None, t, HEAD), lambda bi: (bi, 0, 0)),
                  pl.BlockSpec((1, HEAD), lambda bi: (0, 0))],
        out_specs=pl.BlockSpec((None, t, FOX_W), lambda bi: (bi, 0, 0)),
        out_shape=jax.ShapeDtypeStruct((b, t, FOX_W), BF16),
        scratch_shapes=[pltpu.VMEM((t, HEAD), F32), pltpu.VMEM((HEAD, t), F32), pltpu.VMEM((t, 2 * FOX_W), BF16)],
        compiler_params=_params(("parallel",)),
        name="fox_mixer",
    )(proj, proj, proj, gates, fb)


OUT_ROWS = 128


def _outproj_kernel(og_ref, ol_ref, of_ref, x_ref, w_ref, npost_ref, npre_ref, wup_ref, x1_ref, h_ref, wupb_ref,
                    cat_ref):
    wupb_ref[...] = wup_ref[...].astype(wupb_ref.dtype)
    cat_ref[:, :GDN_W] = og_ref[...]
    cat_ref[:, GDN_W:GDN_W + GLA_W] = ol_ref[...]
    cat_ref[:, GDN_W + GLA_W:] = of_ref[...]
    w = w_ref[...]
    for r in range(cat_ref.shape[0] // OUT_ROWS):
        rows = slice(r * OUT_ROWS, (r + 1) * OUT_ROWS)
        x1 = x_ref[rows, :] + _rms(_dot(cat_ref[rows, :], w), npost_ref[...])
        x1_ref[rows, :] = x1
        h_ref[rows, :] = _rms(x1, npre_ref[...]).astype(h_ref.dtype)


def _outproj(o_gdn, o_gla, o_fox, x, w_out, layer, n_post, n_pre_next, w_up, tm=512):
    n, d = x.shape
    steps = n // tm
    wrows, wcols = w_up.shape[1] // steps, w_up.shape[2]
    rowblk = lambda wd: pl.BlockSpec((tm, wd), lambda i: (i, 0))
    const = lambda shape: pl.BlockSpec(shape, lambda i: (0, 0))
    return pl.pallas_call(
        _outproj_kernel,
        grid=(steps,),
        in_specs=[rowblk(GDN_W), rowblk(GLA_W), rowblk(FOX_W), rowblk(d),
                  pl.BlockSpec((None, d, d), lambda i: (layer, 0, 0), pipeline_mode=pl.Buffered(1)),
                  const((1, d)), const((1, d)), pl.BlockSpec((None, wrows, wcols), lambda i: (layer, i, 0))],
        out_specs=[rowblk(d), rowblk(d), pl.BlockSpec((wrows, wcols), lambda i: (i, 0))],
        out_shape=[jax.ShapeDtypeStruct((n, d), F32), jax.ShapeDtypeStruct((n, d), BF16),
                   jax.ShapeDtypeStruct(w_up.shape[1:], BF16)],
        scratch_shapes=[pltpu.VMEM((tm, d), BF16)],
        compiler_params=_params(("parallel",)),
        name="out_proj",
    )(o_gdn, o_gla, o_fox, x, w_out, n_post.reshape(1, d), n_pre_next.reshape(1, d), w_up)


def _gelu_tanh(x):
    k = 2.0 * math.sqrt(2.0 / math.pi)
    return x / (1.0 + jnp.exp(x * ((-k * 0.044715) * (x * x) - k)))


FFN_ROWS = 256


def _ffn_up_kernel(h_ref, wg_ref, wv_ref, cg_ref, cv_ref, bg_ref, bv_ref, o_ref):
    seq, tf = o_ref.shape
    wg = wg_ref[...]
    wv = wv_ref[...]

    def conv(u, tail, c_ref, b_ref):
        c = c_ref[...]
        ue = jnp.concatenate([tail, u], axis=0)
        return (u * c[2:3, :] + pltpu.roll(ue, 1, 0)[8:, :] * c[1:2, :] + pltpu.roll(ue, 2, 0)[8:, :] * c[0:1, :]
                + b_ref[...])

    tail_g = jnp.zeros((8, tf), F32)
    tail_v = jnp.zeros((8, tf), F32)
    for r in range(seq // FFN_ROWS):
        rows = slice(r * FFN_ROWS, (r + 1) * FFN_ROWS)
        h = h_ref[rows, :]
        ug = _dot(h, wg)
        uv = _dot(h, wv)
        o_ref[rows, :] = (_gelu_tanh(conv(ug, tail_g, cg_ref, bg_ref)) * conv(uv, tail_v, cv_ref, bv_ref)).astype(o_ref.dtype)
        tail_g, tail_v = ug[FFN_ROWS - 8:, :], uv[FFN_ROWS - 8:, :]


def _ffn_up(h, w_up, conv_w, conv_b, layer, seq, tf=1024):
    n, d = h.shape
    nf = D_FF // tf
    conv_b = conv_b.reshape(conv_b.shape[0], 1, -1)
    wspec = lambda off: pl.BlockSpec((d, tf), lambda i, j: (0, off + j))
    cspec = lambda rows, off: pl.BlockSpec((None, rows, tf), lambda i, j: (layer, 0, off + j))
    return pl.pallas_call(
        _ffn_up_kernel,
        grid=(n // seq, nf),
        in_specs=[pl.BlockSpec((seq, d), lambda i, j: (i, 0)), wspec(0), wspec(nf),
                  cspec(3, 0), cspec(3, nf), cspec(1, 0), cspec(1, nf)],
        out_specs=pl.BlockSpec((seq, tf), lambda i, j: (i, j)),
        out_shape=jax.ShapeDtypeStruct((n, D_FF), BF16),
        compiler_params=_params(("parallel", "parallel")),
        name="ffn_up",
    )(h, w_up, w_up, conv_w, conv_w, conv_b, conv_b)


def _ffn_down_kernel(a_ref, w_ref, x_ref, npost_ref, *rest):
    x2 = x_ref[...] + _rms(_dot(a_ref[...], w_ref[...]), npost_ref[...])
    if len(rest) == 1:
        rest[0][...] = x2
    else:
        nnext_ref, x2_ref, h_ref = rest
        x2_ref[...] = x2
        h_ref[...] = _rms(x2, nnext_ref[...]).astype(h_ref.dtype)


def _ffn_down(a, w_down, layer, x, n_post, n_next=None, tm=256):
    n, d = x.shape
    kdim = a.shape[1]
    const = pl.BlockSpec((1, d), lambda i: (0, 0))
    rowblk = pl.BlockSpec((tm, d), lambda i: (i, 0))
    in_specs = [pl.BlockSpec((tm, kdim), lambda i: (i, 0)),
                pl.BlockSpec((kdim, d), lambda i: (0, 0), pipeline_mode=pl.Buffered(1)), rowblk, const]
    args = [a, w_down, x, n_post.reshape(1, d)]
    x_out = jax.ShapeDtypeStruct((n, d), F32)
    if n_next is None:
        out_specs, out_shape = rowblk, x_out
    else:
        in_specs.append(const)
        args.append(n_next.reshape(1, d))
        out_specs, out_shape = [rowblk, rowblk], [x_out, jax.ShapeDtypeStruct((n, d), BF16)]
    return pl.pallas_call(
        _ffn_down_kernel,
        grid=(n // tm,),
        in_specs=in_specs,
        out_specs=out_specs,
        out_shape=out_shape,
        compiler_params=_params(("parallel",)),
        name="ffn_down",
    )(*args)


def _pack_in_proj(w_in):
    wt = jnp.swapaxes(w_in, 1, 2)
    depth, _, d = wt.shape
    o = 0
    segs = {}
    for name, width in (("gdn", 4 * GDN_W), ("gdn_b", H_GDN), ("gdn_a", H_GDN), ("gla_q", H_GLA * GLA_DK),
                        ("gla_k", H_GLA * GLA_DK), ("gla_vg", 2 * GLA_W), ("gla_lr", GLA_RANK),
                        ("fox", 3 * FOX_W), ("fox_f", H_FOX)):
        segs[name] = wt[:, o:o + width]
        o += width
    per_head = lambda m: m.reshape(depth, H_GLA, GLA_DK, d)
    gla_qk = jnp.concatenate([per_head(segs["gla_q"]), per_head(segs["gla_k"])], axis=2).reshape(depth, GLA_W, d)
    zeros = lambda n: jnp.zeros((depth, n, d), wt.dtype)
    qkv = segs["gdn"][:, :QKV_W]
    rest = jnp.concatenate([gla_qk, segs["gla_vg"], segs["fox"], segs["gdn"][:, QKV_W:]], axis=1)
    small = jnp.concatenate([segs["gdn_b"], segs["gdn_a"], segs["gla_lr"], segs["fox_f"],
                             zeros(HEAD - 2 * H_GDN - GLA_RANK - H_FOX)], axis=1)
    return qkv.astype(BF16), rest.astype(BF16), small.astype(BF16)


def kernel(x, w_in, conv_gdn, gdn_a_log, gdn_dt_bias, gdn_norm, gla_w_gate, gla_b_gate, gla_norm, fox_f_bias, w_out, norm_pre_mix, norm_post_mix, norm_pre_ffn, norm_post_ffn, w_up, conv_ffn, conv_ffn_bias, w_down):
    b, t, d = x.shape
    n = b * t
    depth = w_in.shape[0]
    xf = x.reshape(n, d)
    w_qkv, w_rest, w_small = _pack_in_proj(w_in)
    w_out_b = w_out.astype(BF16)
    h = _rmsnorm(xf, norm_pre_mix[0])
    for i in range(depth):
        qkv, w_down_b = _in_proj_conv(h, w_qkv, conv_gdn, i, seq=t, w_down=w_down)
        qkv = qkv.reshape(b, t, QKV_W)
        rest = _matmul_nt(h, w_rest, i, BF16, tm=1024, tn=REST_W // 3, name="in_proj").reshape(b, t, REST_W)
        gates = _matmul_nt(h, w_small, i, F32, tm=1024, tn=HEAD, name="in_proj_gates").reshape(b, t, HEAD)
        o_gdn = _gdn_mixer(qkv, rest, gates, gdn_a_log[i], gdn_dt_bias[i], gdn_norm[i])
        o_gla = _gla_mixer(rest, gates, gla_w_gate[i], gla_b_gate[i], gla_norm[i])
        o_fox = _fox_mixer(rest, gates, fox_f_bias[i])
        xf, h, w_up_b = _outproj(o_gdn.reshape(n, GDN_W), o_gla.reshape(n, GLA_W), o_fox.reshape(n, FOX_W), xf,
                                 w_out_b, i, norm_post_mix[i], norm_pre_ffn[i], w_up)
        a = _ffn_up(h, w_up_b, conv_ffn, conv_ffn_bias, i, seq=t)
        if i + 1 < depth:
            xf, h = _ffn_down(a, w_down_b, i, xf, norm_post_ffn[i], norm_pre_mix[i + 1])
        else:
            xf = _ffn_down(a, w_down_b, i, xf, norm_post_ffn[i])
    return xf.reshape(b, t, d)
```
